```python
import jax, jax.numpy as jnp
from jax import lax
import numpy as np

D_MODEL = 4096
BATCH = 4
SEQ = 4096
DEPTH = 4

CHUNK = 64
N_META = 16
Q_BLOCK = 128
MLA_HEADS = D_MODEL // 256
NOPE_DIM = 128
ROPE_DIM = 64
QK_DIM = NOPE_DIM + ROPE_DIM
V_DIM = 128
MLA_WIDTH = MLA_HEADS * V_DIM
Q_LORA = D_MODEL // 4
KV_LORA = D_MODEL // 8
POOL_WINDOWS = (2, 4, 8, 16)
POOL_GROUPS = 4
POOL_WIDTH = D_MODEL // 2
POOL_GROUP_DIM = POOL_WIDTH // POOL_GROUPS
ROPE_THETA = 10000.0
NORM_EPS = 1e-6
IN_SPLITS = (Q_LORA, KV_LORA, ROPE_DIM, POOL_WIDTH, MLA_WIDTH, POOL_WIDTH, D_MODEL, D_MODEL)
IN_WIDTH = Q_LORA + KV_LORA + ROPE_DIM + POOL_WIDTH + MLA_WIDTH + POOL_WIDTH + 2 * D_MODEL

kernel_name = "hybrid_mla_pool_gated_merge_trunk"


def _rms(x, g):
    xf = x.astype(jnp.float32)
    y = xf * lax.rsqrt(jnp.mean(xf * xf, axis=-1, keepdims=True) + NORM_EPS)
    return (y * g.astype(jnp.float32)).astype(x.dtype)


def _rope_tables(length):
    pos = jnp.arange(length, dtype=jnp.float32)
    inv = 1.0 / (ROPE_THETA ** (jnp.arange(0, ROPE_DIM, 2, dtype=jnp.float32) / ROPE_DIM))
    ang = pos[:, None] * inv[None, :]
    return jnp.cos(ang)[:, None, :], jnp.sin(ang)[:, None, :]


def _apply_rope(x, cos, sin):
    xf = x.astype(jnp.float32)
    x1, x2 = jnp.split(xf, 2, axis=-1)
    return jnp.concatenate([x1 * cos - x2 * sin, x2 * cos + x1 * sin], axis=-1).astype(x.dtype)


def _attend_block(q_blk, cid_q, k, v, cid_k):
    s = jnp.einsum('bqhd,bkhd->bhqk', q_blk, k, preferred_element_type=jnp.float32) * (QK_DIM ** -0.5)
    mask = cid_k[None, :] <= cid_q[:, None]
    s = jnp.where(mask[None, None], s, jnp.float32(-1e30))
    p = jax.nn.softmax(s, axis=-1).astype(v.dtype)
    return jnp.einsum('bhqk,bkhd->bqhd', p, v)


def _mla(c_q_raw, c_kv_raw, k_rope_raw, q_lora_g, kv_lora_g, w_uq, w_ukv, q_head_g, k_head_g, cos, sin, cid):
    b, length, _ = c_q_raw.shape
    c_q = _rms(c_q_raw, q_lora_g)
    c_kv = _rms(c_kv_raw, kv_lora_g)
    q = (c_q @ w_uq).reshape(b, length, MLA_HEADS, QK_DIM)
    kv = (c_kv @ w_ukv).reshape(b, length, MLA_HEADS, NOPE_DIM + V_DIM)
    k_nope, v = kv[..., :NOPE_DIM], kv[..., NOPE_DIM:]
    k_rope = jnp.broadcast_to(k_rope_raw[:, :, None, :], (b, length, MLA_HEADS, ROPE_DIM))
    k = jnp.concatenate([k_nope, k_rope], axis=-1)
    q = _rms(q, q_head_g)
    k = _rms(k, k_head_g)
    q = jnp.concatenate([q[..., :NOPE_DIM], _apply_rope(q[..., NOPE_DIM:], cos, sin)], axis=-1)
    k = jnp.concatenate([k[..., :NOPE_DIM], _apply_rope(k[..., NOPE_DIM:], cos, sin)], axis=-1)
    out_meta = _attend_block(q[:, :N_META], cid[:N_META], k, v, cid)
    n_blk = (length - N_META) // Q_BLOCK
    q_blocks = jnp.moveaxis(q[:, N_META:].reshape(b, n_blk, Q_BLOCK, MLA_HEADS, QK_DIM), 1, 0)
    cid_blocks = cid[N_META:].reshape(n_blk, Q_BLOCK)
    out_blocks = lax.map(lambda a: _attend_block(a[0], a[1], k, v, cid), (q_blocks, cid_blocks))
    out_real = jnp.moveaxis(out_blocks, 0, 1).reshape(b, length - N_META, MLA_HEADS, V_DIM)
    out = jnp.concatenate([out_meta, out_real], axis=1)
    return out.reshape(b, length, MLA_WIDTH)


def _pool_mixer(u, w_pool, pool_scale):
    b, length, _ = u.shape
    uf = u.astype(jnp.float32)
    cs = jnp.cumsum(uf, axis=1)
    pos1 = jnp.arange(length, dtype=jnp.float32) + 1.0
    pooled = []
    for g, w in enumerate(POOL_WINDOWS):
        c = cs[..., g * POOL_GROUP_DIM:(g + 1) * POOL_GROUP_DIM]
        prev = jnp.pad(c, ((0, 0), (w, 0), (0, 0)))[:, :length]
        count = jnp.minimum(pos1, jnp.float32(w))[None, :, None]
        pooled.append((c - prev) / count)
    pooled = jnp.concatenate(pooled, axis=-1)
    diff = (pooled - uf).astype(u.dtype).reshape(b, length, POOL_GROUPS, POOL_GROUP_DIM)
    mixed = jnp.einsum('blgc,gcd->blgd', diff, w_pool).reshape(b, length, POOL_WIDTH)
    return mixed * pool_scale


def setup_inputs(seed: int = 0) -> dict:
    key = jax.random.key(seed)
    ks = jax.random.split(key, 20)
    f32 = jnp.float32

    def w(k, shape, fan_in):
        return jax.random.normal(k, shape, f32) * (fan_in ** -0.5)

    def gain(k, shape):
        return 1.0 + 0.05 * jax.random.normal(k, shape, f32)

    return {
        "x": jax.random.normal(ks[0], (BATCH, SEQ, D_MODEL), f32),
        "meta_tokens": jax.random.normal(ks[1], (N_META, D_MODEL), f32),
        "norm_g": gain(ks[2], (DEPTH, D_MODEL)),
        "w_in": w(ks[3], (DEPTH, D_MODEL, IN_WIDTH), D_MODEL),
        "q_lora_g": gain(ks[4], (DEPTH, Q_LORA)),
        "kv_lora_g": gain(ks[5], (DEPTH, KV_LORA)),
        "w_uq": w(ks[6], (DEPTH, Q_LORA, MLA_HEADS * QK_DIM), Q_LORA),
        "w_ukv": w(ks[7], (DEPTH, KV_LORA, MLA_HEADS * (NOPE_DIM + V_DIM)), KV_LORA),
        "q_head_g": gain(ks[8], (DEPTH, QK_DIM)),
        "k_head_g": gain(ks[9], (DEPTH, QK_DIM)),
        "w_pool": w(ks[10], (DEPTH, POOL_GROUPS, POOL_GROUP_DIM, POOL_GROUP_DIM), POOL_GROUP_DIM),
        "pool_scale": 1.0 + 0.1 * jax.random.normal(ks[11], (DEPTH, POOL_WIDTH), f32),
        "w_branch_a": w(ks[12], (DEPTH, MLA_WIDTH, D_MODEL), MLA_WIDTH),
        "w_branch_b": w(ks[13], (DEPTH, POOL_WIDTH, D_MODEL), POOL_WIDTH),
        "w_out": w(ks[14], (DEPTH, D_MODEL, D_MODEL), D_MODEL),
    }


def reference(x, meta_tokens, norm_g, w_in, q_lora_g, kv_lora_g, w_uq, w_ukv, q_head_g, k_head_g,
              w_pool, pool_scale, w_branch_a, w_branch_b, w_out):
    b, seq, _ = x.shape
    length = N_META + seq
    h = jnp.concatenate([jnp.broadcast_to(meta_tokens[None].astype(x.dtype), (b, N_META, D_MODEL)), x], axis=1)
    cid = jnp.concatenate([jnp.zeros((N_META,), jnp.int32),
                           jnp.arange(seq, dtype=jnp.int32) // CHUNK + 1])
    cos, sin = _rope_tables(length)
    bounds = []
    acc = 0
    for s in IN_SPLITS[:-1]:
        acc += s
        bounds.append(acc)
    for l in range(DEPTH):
        hn = _rms(h, norm_g[l])
        proj = hn @ w_in[l]
        c_q_raw, c_kv_raw, k_rope_raw, pool_in, gate_a, gate_b, merge_a, merge_b = jnp.split(proj, bounds, axis=-1)
        attn = _mla(c_q_raw, c_kv_raw, k_rope_raw, q_lora_g[l], kv_lora_g[l], w_uq[l], w_ukv[l],
                    q_head_g[l], k_head_g[l], cos, sin, cid)
        br_a = (attn * jax.nn.silu(gate_a)) @ w_branch_a[l]
        pooled = _pool_mixer(pool_in, w_pool[l], pool_scale[l])
        br_b = (pooled * jax.nn.silu(gate_b)) @ w_branch_b[l]
        merged = jax.nn.sigmoid(merge_a) * br_a + jax.nn.sigmoid(merge_b) * br_b
        h = h + merged @ w_out[l]
    return h[:, N_META:]
```

```python
import functools

import jax
import jax.numpy as jnp
from jax import lax
from jax.experimental import pallas as pl
from jax.experimental.pallas import tpu as pltpu

F32 = jnp.float32
BF16 = jnp.bfloat16

D_MODEL = 4096
BATCH = 4
SEQ = 4096
DEPTH = 4
CHUNK = 64
N_META = 16
HEADS = 16
NOPE = 128
ROPE = 64
QK = NOPE + ROPE
V_DIM = 128
MLA_W = HEADS * V_DIM
Q_LORA = 1024
KV_LORA = 512
POOL_W = 2048
WINDOWS = (2, 4, 8, 16)
GROUPS = len(WINDOWS)
GROUP_W = POOL_W // GROUPS
NORM_EPS = 1e-6
ROPE_THETA = 10000.0

LANES = 128
MXU_COLS = 256
VMEM_BYTES = 64 * 1024 * 1024

N_REAL = BATCH * SEQ
N_ROWS = N_REAL + N_META
META_BLK = N_REAL // N_META
QK_PAD = MXU_COLS
LAT_W = Q_LORA + KV_LORA + LANES
MAIN_W = POOL_W + MLA_W + POOL_W + 2 * D_MODEL
LAT_SRC = Q_LORA + KV_LORA + ROPE

TM_MM = 656
TM_ROW = 400
TN = 1024
TQ = 512
TP = 256


def _vmem_limit(block_bytes, temp_bytes):
    need = 2 * sum(block_bytes) + temp_bytes
    assert need <= VMEM_BYTES - (4 << 20), need
    return int(need)


def _nbytes(shape, dtype):
    n = 1
    for s in shape:
        n *= s
    return n * jnp.dtype(dtype).itemsize


def _rmsnorm_kernel(h_ref, g_ref, o_ref):
    x = h_ref[...]
    ms = jnp.mean(x * x, axis=-1, keepdims=True)
    o_ref[...] = (x * lax.rsqrt(ms + NORM_EPS) * g_ref[...]).astype(o_ref.dtype)


def _rmsnorm(h, norm_g, layer):
    n_tiles = N_ROWS // TM_ROW
    blocks = [_nbytes((TM_ROW, D_MODEL), F32), _nbytes((TM_ROW, D_MODEL), BF16)]
    return pl.pallas_call(
        _rmsnorm_kernel,
        out_shape=jax.ShapeDtypeStruct((N_ROWS, D_MODEL), BF16),
        grid=(n_tiles,),
        in_specs=[
            pl.BlockSpec((TM_ROW, D_MODEL), lambda i: (i, 0)),
            pl.BlockSpec((None, 1, D_MODEL), lambda i: (layer, 0, 0)),
        ],
        out_specs=pl.BlockSpec((TM_ROW, D_MODEL), lambda i: (i, 0)),
        compiler_params=pltpu.CompilerParams(
            dimension_semantics=("parallel",),
            vmem_limit_bytes=_vmem_limit(blocks, _nbytes((TM_ROW, D_MODEL), F32))),
        name="rmsnorm",
    )(h, norm_g)


_RAW_TILES = POOL_W // TN
_SILU_TILES = (MLA_W + POOL_W) // TN


def _inproj_kernel(x_ref, w_ref, o_ref):
    j = pl.program_id(1)
    acc = jnp.dot(x_ref[...], w_ref[...], preferred_element_type=F32)
    sig = jax.nn.sigmoid(acc)
    out = jnp.where(j < _RAW_TILES, acc,
                    jnp.where(j < _RAW_TILES + _SILU_TILES, acc * sig, sig))
    o_ref[...] = out.astype(o_ref.dtype)


def _inproj(hn, w_main, layer):
    blocks = [_nbytes((TM_MM, D_MODEL), BF16), _nbytes((D_MODEL, TN), BF16),
              _nbytes((TM_MM, TN), BF16)]
    return pl.pallas_call(
        _inproj_kernel,
        out_shape=jax.ShapeDtypeStruct((N_ROWS, MAIN_W), BF16),
        grid=(N_ROWS // TM_MM, MAIN_W // TN),
        in_specs=[
            pl.BlockSpec((TM_MM, D_MODEL), lambda i, j: (i, 0)),
            pl.BlockSpec((None, D_MODEL, TN), lambda i, j: (layer, 0, j)),
        ],
        out_specs=pl.BlockSpec((TM_MM, TN), lambda i, j: (i, j)),
        compiler_params=pltpu.CompilerParams(
            dimension_semantics=("parallel", "parallel"),
            vmem_limit_bytes=_vmem_limit(blocks, 3 * _nbytes((TM_MM, TN), F32))),
        name="inproj",
    )(hn, w_main)


def _latent_kernel(x_ref, w_ref, o_ref):
    o_ref[...] = jnp.dot(x_ref[...], w_ref[...], preferred_element_type=F32)


def _latent(hn, w_lat, layer):
    blocks = [_nbytes((TM_ROW, D_MODEL), BF16), _nbytes((D_MODEL, LAT_W), BF16),
              _nbytes((TM_ROW, LAT_W), F32)]
    return pl.pallas_call(
        _latent_kernel,
        out_shape=jax.ShapeDtypeStruct((N_ROWS, LAT_W), F32),
        grid=(N_ROWS // TM_ROW,),
        in_specs=[
            pl.BlockSpec((TM_ROW, D_MODEL), lambda i: (i, 0)),
            pl.BlockSpec((None, D_MODEL, LAT_W), lambda i: (layer, 0, 0)),
        ],
        out_specs=pl.BlockSpec((TM_ROW, LAT_W), lambda i: (i, 0)),
        compiler_params=pltpu.CompilerParams(
            dimension_semantics=("parallel",),
            vmem_limit_bytes=_vmem_limit(blocks, _nbytes((TM_ROW, LAT_W), F32))),
        name="latent",
    )(hn, w_lat)


def _rope(x, cos, sin_signed):
    partner = pltpu.roll(x, ROPE // 2, axis=1) + pltpu.roll(x, LANES - ROPE // 2, axis=1)
    return x * cos + partner * sin_signed


def _mla_prep_kernel(lat_ref, gql_ref, gkvl_ref, wuq_ref, wukv_ref, gq_ref, gk_ref,
                     cos_ref, sin_ref, q_ref, k_ref, v_ref):
    cos = cos_ref[...]
    sin = sin_ref[...]
    cq = lat_ref[:, 0:Q_LORA]
    ckv = lat_ref[:, Q_LORA:Q_LORA + KV_LORA]
    kr = lat_ref[:, Q_LORA + KV_LORA:LAT_W]
    cqn = (cq * lax.rsqrt(jnp.mean(cq * cq, axis=-1, keepdims=True) + NORM_EPS)
           * gql_ref[...]).astype(BF16)
    ckvn = (ckv * lax.rsqrt(jnp.mean(ckv * ckv, axis=-1, keepdims=True) + NORM_EPS)
            * gkvl_ref[...]).astype(BF16)
    gq = gq_ref[...]
    gk = gk_ref[...]
    kr_ssq = jnp.sum(kr * kr, axis=-1, keepdims=True)
    kr_roped = _rope(kr * gk[:, NOPE:QK_PAD], cos, sin)
    for h in range(HEADS):
        c0 = h * QK_PAD
        qh = jnp.dot(cqn, wuq_ref[:, c0:c0 + QK_PAD], preferred_element_type=F32)
        rq = lax.rsqrt(jnp.sum(qh * qh, axis=-1, keepdims=True) * (1.0 / QK) + NORM_EPS)
        qn = qh * rq * gq
        q_ref[:, c0:c0 + NOPE] = qn[:, 0:NOPE].astype(BF16)
        q_ref[:, c0 + NOPE:c0 + QK_PAD] = _rope(qn[:, NOPE:QK_PAD], cos, sin).astype(BF16)
        kvh = jnp.dot(ckvn, wukv_ref[:, c0:c0 + QK_PAD], preferred_element_type=F32)
        kn = kvh[:, 0:NOPE]
        rk = lax.rsqrt((jnp.sum(kn * kn, axis=-1, keepdims=True) + kr_ssq) * (1.0 / QK)
                       + NORM_EPS)
        k_ref[:, c0:c0 + NOPE] = (kn * rk * gk[:, 0:NOPE]).astype(BF16)
        k_ref[:, c0 + NOPE:c0 + QK_PAD] = (kr_roped * rk).astype(BF16)
        v_ref[:, h * V_DIM:(h + 1) * V_DIM] = kvh[:, NOPE:NOPE + V_DIM].astype(BF16)


def _mla_prep(lat, gql, gkvl, wuq, wukv, gq, gk, cos, sin, layer):
    blocks = [_nbytes((TM_ROW, LAT_W), F32), _nbytes((Q_LORA, HEADS * QK_PAD), BF16),
              _nbytes((KV_LORA, HEADS * QK_PAD), BF16),
              2 * _nbytes((TM_ROW, HEADS * QK_PAD), BF16), _nbytes((TM_ROW, MLA_W), BF16),
              2 * _nbytes((TM_ROW, LANES), F32)]
    row = lambda i: (i, 0)
    par = lambda i: (layer, 0, 0)
    return pl.pallas_call(
        _mla_prep_kernel,
        out_shape=(jax.ShapeDtypeStruct((N_ROWS, HEADS * QK_PAD), BF16),
                   jax.ShapeDtypeStruct((N_ROWS, HEADS * QK_PAD), BF16),
                   jax.ShapeDtypeStruct((N_ROWS, MLA_W), BF16)),
        grid=(N_ROWS // TM_ROW,),
        in_specs=[
            pl.BlockSpec((TM_ROW, LAT_W), row),
            pl.BlockSpec((None, 1, Q_LORA), par),
            pl.BlockSpec((None, 1, KV_LORA), par),
            pl.BlockSpec((None, Q_LORA, HEADS * QK_PAD), par),
            pl.BlockSpec((None, KV_LORA, HEADS * QK_PAD), par),
            pl.BlockSpec((None, 1, QK_PAD), par),
            pl.BlockSpec((None, 1, QK_PAD), par),
            pl.BlockSpec((TM_ROW, LANES), row),
            pl.BlockSpec((TM_ROW, LANES), row),
        ],
        out_specs=(pl.BlockSpec((TM_ROW, HEADS * QK_PAD), row),
                   pl.BlockSpec((TM_ROW, HEADS * QK_PAD), row),
                   pl.BlockSpec((TM_ROW, MLA_W), row)),
        compiler_params=pltpu.CompilerParams(
            dimension_semantics=("parallel",),
            vmem_limit_bytes=_vmem_limit(
                blocks, _nbytes((TM_ROW, Q_LORA + KV_LORA), BF16)
                + 6 * _nbytes((TM_ROW, QK_PAD), F32))),
        name="mla_prep",
    )(lat, gql, gkvl, wuq, wukv, gq, gk, cos, sin)


_NT = (((1,), (1,)), ((), ()))
_MASKED = -1e30


def _softmax_step(s, v, m, l, acc):
    m_new = jnp.maximum(m, jnp.max(s, axis=-1, keepdims=True))
    alpha = jnp.exp(m - m_new)
    p = jnp.exp(s - m_new)
    l = alpha * l + jnp.sum(p, axis=-1, keepdims=True)
    acc = alpha * acc + jnp.dot(p.astype(BF16), v, preferred_element_type=F32)
    return m_new, l, acc


def _attn_kernel(q_ref, k_ref, v_ref, km_ref, vm_ref, g_ref, o_ref, *, real_keys):
    q = q_ref[...]
    rows = q.shape[0]
    pad = LANES - N_META
    km = jnp.concatenate([km_ref[...], jnp.zeros((pad, QK_PAD), BF16)], axis=0)
    vm = jnp.concatenate([vm_ref[...], jnp.zeros((pad, V_DIM), BF16)], axis=0)
    s = lax.dot_general(q, km, _NT, preferred_element_type=F32)
    s = jnp.where(lax.broadcasted_iota(jnp.int32, s.shape, 1) < N_META, s, _MASKED)
    m = jnp.max(s, axis=-1, keepdims=True)
    p = jnp.exp(s - m)
    l = jnp.sum(p, axis=-1, keepdims=True)
    acc = jnp.dot(p.astype(BF16), vm, preferred_element_type=F32)

    if real_keys:
        i = pl.program_id(2)

        def body(j, carry):
            start = pl.multiple_of(j * TQ, TQ)
            s = lax.dot_general(q, k_ref[pl.ds(start, TQ), :], _NT,
                                preferred_element_type=F32)
            return _softmax_step(s, v_ref[pl.ds(start, TQ), :], *carry)

        m, l, acc = lax.fori_loop(0, i, body, (m, l, acc))
        start = pl.multiple_of(i * TQ, TQ)
        s = lax.dot_general(q, k_ref[pl.ds(start, TQ), :], _NT, preferred_element_type=F32)
        q_chunk = lax.broadcasted_iota(jnp.int32, s.shape, 0) // CHUNK
        k_chunk = lax.broadcasted_iota(jnp.int32, s.shape, 1) // CHUNK
        s = jnp.where(k_chunk <= q_chunk, s, _MASKED)
        m, l, acc = _softmax_step(s, v_ref[pl.ds(start, TQ), :], m, l, acc)

    del rows
    o_ref[...] = (acc / l * g_ref[...].astype(F32)).astype(o_ref.dtype)


_GATE_A_BLK = POOL_W // V_DIM


def _attention(q, k, v, proj, gated_prev=None):
    nq = SEQ // TQ
    if gated_prev is None:
        blocks = [_nbytes((TQ, QK_PAD), BF16), _nbytes((SEQ, QK_PAD), BF16),
                  _nbytes((SEQ, V_DIM), BF16), _nbytes((N_META, QK_PAD + V_DIM), BF16),
                  2 * _nbytes((TQ, V_DIM), BF16)]
        return pl.pallas_call(
            functools.partial(_attn_kernel, real_keys=True),
            out_shape=jax.ShapeDtypeStruct((N_ROWS, MLA_W), BF16),
            grid=(BATCH, HEADS, nq),
            in_specs=[
                pl.BlockSpec((TQ, QK_PAD), lambda b, h, i: (b * nq + i, h)),
                pl.BlockSpec((SEQ, QK_PAD), lambda b, h, i: (b, h)),
                pl.BlockSpec((SEQ, V_DIM), lambda b, h, i: (b, h)),
                pl.BlockSpec((N_META, QK_PAD), lambda b, h, i: (META_BLK, h)),
                pl.BlockSpec((N_META, V_DIM), lambda b, h, i: (META_BLK, h)),
                pl.BlockSpec((TQ, V_DIM), lambda b, h, i: (b * nq + i, _GATE_A_BLK + h)),
            ],
            out_specs=pl.BlockSpec((TQ, V_DIM), lambda b, h, i: (b * nq + i, h)),
            compiler_params=pltpu.CompilerParams(
                dimension_semantics=("parallel", "parallel", "parallel"),
                vmem_limit_bytes=_vmem_limit(blocks, 6 * _nbytes((TQ, TQ), F32))),
            name="attention",
        )(q, k, v, k, v, proj)
    blocks = [_nbytes((N_META, QK_PAD), BF16), _nbytes((N_META, QK_PAD + V_DIM), BF16),
              2 * _nbytes((N_META, V_DIM), BF16)]
    meta = lambda h: (META_BLK, h)
    return pl.pallas_call(
        lambda q_ref, km_ref, vm_ref, g_ref, prev_ref, o_ref: _attn_kernel(
            q_ref, None, None, km_ref, vm_ref, g_ref, o_ref, real_keys=False),
        out_shape=jax.ShapeDtypeStruct((N_ROWS, MLA_W), BF16),
        grid=(HEADS,),
        in_specs=[
            pl.BlockSpec((N_META, QK_PAD), meta),
            pl.BlockSpec((N_META, QK_PAD), meta),
            pl.BlockSpec((N_META, V_DIM), meta),
            pl.BlockSpec((N_META, V_DIM), lambda h: (META_BLK, _GATE_A_BLK + h)),
            pl.BlockSpec(memory_space=pl.ANY),
        ],
        out_specs=pl.BlockSpec((N_META, V_DIM), meta),
        input_output_aliases={4: 0},
        compiler_params=pltpu.CompilerParams(
            dimension_semantics=("parallel",),
            vmem_limit_bytes=_vmem_limit(blocks, 8 * _nbytes((LANES, QK_PAD), F32))),
        name="attention_meta",
    )(q, k, v, proj, gated_prev)


def _pool_kernel(u_ref, halo_ref, bm_ref, bh_ref, invc_ref, wp_ref, sc_ref, g_ref, o_ref,
                 *, has_main):
    pad = LANES - N_META
    for g in range(GROUPS):
        cs = slice(g * GROUP_W, (g + 1) * GROUP_W)
        halo = jnp.concatenate([halo_ref[:, cs], jnp.zeros((pad, GROUP_W), BF16)], axis=0)
        win = jnp.dot(bh_ref[g], halo, preferred_element_type=F32)
        u = u_ref[:, cs]
        if has_main:
            win = win + jnp.dot(bm_ref[g], u, preferred_element_type=F32)
        diff = win * invc_ref[g][:, 0:1] - u.astype(F32)
        mixed = jnp.dot(diff.astype(BF16), wp_ref[g], preferred_element_type=F32)
        out = mixed * sc_ref[:, cs] * g_ref[:, cs].astype(F32)
        o_ref[:, cs] = out.astype(o_ref.dtype)


_GATE_B_BLK = (POOL_W + MLA_W) // POOL_W


def _pool_bands(rows, meta):
    t = jnp.arange(rows)[:, None]
    s = jnp.arange(rows)[None, :]
    s_h = jnp.arange(LANES)[None, :]
    main, halo, invc = [], [], []
    for w in WINDOWS:
        main.append(((t - s >= 0) & (t - s < w)).astype(BF16))
        if meta:
            halo.append(((t - s_h >= 0) & (t - s_h < w) & (s_h < N_META)).astype(BF16))
            cnt = jnp.minimum(t + 1, w).astype(F32)
        else:
            dist = t + N_META - s_h
            halo.append(((dist < w) & (s_h < N_META)).astype(BF16))
            cnt = jnp.full((rows, 1), w, F32)
        invc.append(jnp.broadcast_to(1.0 / cnt, (rows, LANES)))
    return jnp.stack(main), jnp.stack(halo), jnp.stack(invc)


def _pool(proj, w_pool, pool_scale, layer, gated_prev=None):
    meta = gated_prev is not None
    rows = N_META if meta else TP
    bm, bh, invc = _pool_bands(rows, meta)
    nt = SEQ // TP
    if meta:
        grid = (1,)
        tile = lambda t: (META_BLK, 0)
        halo = tile
        gate = lambda t: (META_BLK, _GATE_B_BLK)
        const3 = lambda t: (0, 0, 0)
        wmap = lambda t: (layer, 0, 0, 0)
        smap = lambda t: (layer, 0, 0)
    else:
        grid = (BATCH, nt)
        tile = lambda b, t: (b * nt + t, 0)
        halo = lambda b, t: (jnp.where(t == 0, META_BLK,
                                       (b * SEQ + t * TP) // N_META - 1), 0)
        gate = lambda b, t: (b * nt + t, _GATE_B_BLK)
        const3 = lambda b, t: (0, 0, 0)
        wmap = lambda b, t: (layer, 0, 0, 0)
        smap = lambda b, t: (layer, 0, 0)
    blocks = [3 * _nbytes((rows, POOL_W), BF16), _nbytes((N_META, POOL_W), BF16),
              _nbytes((GROUPS, rows, rows + LANES), BF16), _nbytes((GROUPS, rows, LANES), F32),
              _nbytes((GROUPS, GROUP_W, GROUP_W), BF16), _nbytes((1, POOL_W), F32)]
    in_specs = [
        pl.BlockSpec((rows, POOL_W), tile),
        pl.BlockSpec((N_META, POOL_W), halo),
        pl.BlockSpec((GROUPS, rows, rows), const3),
        pl.BlockSpec((GROUPS, rows, LANES), const3),
        pl.BlockSpec((GROUPS, rows, LANES), const3),
        pl.BlockSpec((None, GROUPS, GROUP_W, GROUP_W), wmap),
        pl.BlockSpec((None, 1, POOL_W), smap),
        pl.BlockSpec((rows, POOL_W), gate),
    ]
    args = [proj, proj, bm, bh, invc, w_pool, pool_scale, proj]
    kern = functools.partial(_pool_kernel, has_main=not meta)
    aliases = {}
    if meta:
        in_specs.append(pl.BlockSpec(memory_space=pl.ANY))
        args.append(gated_prev)
        aliases = {8: 0}
        inner = kern
        kern = lambda *refs: inner(*refs[:8], refs[9])
    return pl.pallas_call(
        kern,
        out_shape=jax.ShapeDtypeStruct((N_ROWS, POOL_W), BF16),
        grid=grid,
        in_specs=in_specs,
        out_specs=pl.BlockSpec((rows, POOL_W), tile),
        input_output_aliases=aliases,
        compiler_params=pltpu.CompilerParams(
            dimension_semantics=("parallel",) * len(grid),
            vmem_limit_bytes=_vmem_limit(blocks, 6 * _nbytes((max(rows, LANES), GROUP_W), F32))),
        name="pool_meta" if meta else "pool",
    )(*args)


def _merge_kernel(ga_ref, gb_ref, wa_ref, wb_ref, sa_ref, sb_ref, o_ref):
    a = jnp.dot(ga_ref[...], wa_ref[...], preferred_element_type=F32)
    b = jnp.dot(gb_ref[...], wb_ref[...], preferred_element_type=F32)
    o_ref[...] = (sa_ref[...].astype(F32) * a + sb_ref[...].astype(F32) * b).astype(o_ref.dtype)


_MERGE_A_BLK = (POOL_W + MLA_W + POOL_W) // TN
_MERGE_B_BLK = _MERGE_A_BLK + D_MODEL // TN


def _merge(ga, gb, w_a, w_b, proj, layer):
    blocks = [2 * _nbytes((TM_MM, MLA_W), BF16), 2 * _nbytes((MLA_W, TN), BF16),
              3 * _nbytes((TM_MM, TN), BF16)]
    return pl.pallas_call(
        _merge_kernel,
        out_shape=jax.ShapeDtypeStruct((N_ROWS, D_MODEL), BF16),
        grid=(N_ROWS // TM_MM, D_MODEL // TN),
        in_specs=[
            pl.BlockSpec((TM_MM, MLA_W), lambda i, j: (i, 0)),
            pl.BlockSpec((TM_MM, POOL_W), lambda i, j: (i, 0)),
            pl.BlockSpec((None, MLA_W, TN), lambda i, j: (layer, 0, j)),
            pl.BlockSpec((None, POOL_W, TN), lambda i, j: (layer, 0, j)),
            pl.BlockSpec((TM_MM, TN), lambda i, j: (i, _MERGE_A_BLK + j)),
            pl.BlockSpec((TM_MM, TN), lambda i, j: (i, _MERGE_B_BLK + j)),
        ],
        out_specs=pl.BlockSpec((TM_MM, TN), lambda i, j: (i, j)),
        compiler_params=pltpu.CompilerParams(
            dimension_semantics=("parallel", "parallel"),
            vmem_limit_bytes=_vmem_limit(blocks, 3 * _nbytes((TM_MM, TN), F32))),
        name="merge",
    )(ga, gb, w_a, w_b, proj, proj)


def _outproj_kernel(m_ref, w_ref, h_ref, o_ref):
    o_ref[...] = h_ref[...] + jnp.dot(m_ref[...], w_ref[...], preferred_element_type=F32)


def _outproj(merged, w_out, h, layer, last):
    blocks = [_nbytes((TM_MM, D_MODEL), BF16), _nbytes((D_MODEL, TN), BF16),
              2 * _nbytes((TM_MM, TN), F32)]
    out_rows = N_REAL if last else N_ROWS
    return pl.pallas_call(
        _outproj_kernel,
        out_shape=jax.ShapeDtypeStruct((out_rows, D_MODEL), F32),
        grid=(N_ROWS // TM_MM, D_MODEL // TN),
        in_specs=[
            pl.BlockSpec((TM_MM, D_MODEL), lambda i, j: (i, 0)),
            pl.BlockSpec((None, D_MODEL, TN), lambda i, j: (layer, 0, j)),
            pl.BlockSpec((TM_MM, TN), lambda i, j: (i, j)),
        ],
        out_specs=pl.BlockSpec((TM_MM, TN), lambda i, j: (i, j)),
        input_output_aliases={} if last else {2: 0},
        compiler_params=pltpu.CompilerParams(
            dimension_semantics=("parallel", "parallel"),
            vmem_limit_bytes=_vmem_limit(blocks, 3 * _nbytes((TM_MM, TN), F32))),
        name="outproj",
    )(merged, w_out, h)


def _rope_tables():
    pos = jnp.concatenate([
        jnp.tile(jnp.arange(SEQ, dtype=F32) + N_META, BATCH),
        jnp.arange(N_META, dtype=F32)])
    inv = 1.0 / (ROPE_THETA ** (jnp.arange(0, ROPE, 2, dtype=F32) / ROPE))
    ang = pos[:, None] * inv[None, :]
    zeros = jnp.zeros((N_ROWS, LANES - ROPE), F32)
    cos = jnp.concatenate([jnp.cos(ang), jnp.cos(ang), zeros], axis=1)
    sin = jnp.concatenate([-jnp.sin(ang), jnp.sin(ang), zeros], axis=1)
    return cos, sin


def _pad_heads(w, width):
    lead = w.shape[:-1]
    w = w.reshape(lead + (HEADS, width))
    w = jnp.pad(w, [(0, 0)] * len(lead) + [(0, 0), (0, QK_PAD - width)])
    return w.reshape(lead + (HEADS * QK_PAD,))


def kernel(x, meta_tokens, norm_g, w_in, q_lora_g, kv_lora_g, w_uq, w_ukv, q_head_g, k_head_g,
           w_pool, pool_scale, w_branch_a, w_branch_b, w_out):
    assert x.shape == (BATCH, SEQ, D_MODEL) and x.dtype == F32
    assert w_in.shape == (DEPTH, D_MODEL, LAT_SRC + MAIN_W)

    w_lat = jnp.pad(w_in[:, :, :LAT_SRC], ((0, 0), (0, 0), (0, LAT_W - LAT_SRC))).astype(BF16)
    w_main = w_in[:, :, LAT_SRC:].astype(BF16)
    w_uq_p = _pad_heads(w_uq, QK).astype(BF16)
    w_ukv_b = w_ukv.astype(BF16)
    w_pool_b = w_pool.astype(BF16)
    w_a = w_branch_a.astype(BF16)
    w_b = w_branch_b.astype(BF16)
    w_o = w_out.astype(BF16)
    head_pad = ((0, 0), (0, QK_PAD - QK))
    gq = (jnp.pad(q_head_g, head_pad) * (QK ** -0.5)).reshape(DEPTH, 1, QK_PAD)
    gk = jnp.pad(k_head_g, head_pad).reshape(DEPTH, 1, QK_PAD)
    norm_g3 = norm_g.reshape(DEPTH, 1, D_MODEL)
    gql = q_lora_g.reshape(DEPTH, 1, Q_LORA)
    gkvl = kv_lora_g.reshape(DEPTH, 1, KV_LORA)
    pscale = pool_scale.reshape(DEPTH, 1, POOL_W)
    cos, sin = _rope_tables()

    h = jnp.concatenate([x.reshape(N_REAL, D_MODEL), meta_tokens.astype(F32)], axis=0)
    for layer in range(DEPTH):
        hn = _rmsnorm(h, norm_g3, layer)
        proj = _inproj(hn, w_main, layer)
        lat = _latent(hn, w_lat, layer)
        q, k, v = _mla_prep(lat, gql, gkvl, w_uq_p, w_ukv_b, gq, gk, cos, sin, layer)
        ga = _attention(q, k, v, proj)
        ga = _attention(q, k, v, proj, gated_prev=ga)
        gb = _pool(proj, w_pool_b, pscale, layer)
        gb = _pool(proj, w_pool_b, pscale, layer, gated_prev=gb)
        merged = _merge(ga, gb, w_a, w_b, proj, layer)
        h = _outproj(merged, w_o, h, layer, last=layer == DEPTH - 1)
    return h.reshape(BATCH, SEQ, D_MODEL)
```

```python
import functools

import jax
import jax.numpy as jnp
from jax import lax
from jax.experimental import pallas as pl
from jax.experimental.pallas import tpu as pltpu

F32 = jnp.float32
BF16 = jnp.bfloat16

D_MODEL = 4096
BATCH = 4
SEQ = 4096
DEPTH = 4
CHUNK = 64
N_META = 16
HEADS = 16
NOPE = 128
ROPE = 64
QK = NOPE + ROPE
V_DIM = 128
MLA_W = HEADS * V_DIM
Q_LORA = 1024
KV_LORA = 512
POOL_W = 2048
WINDOWS = (2, 4, 8, 16)
GROUPS = len(WINDOWS)
GROUP_W = POOL_W // GROUPS
NORM_EPS = 1e-6
ROPE_THETA = 10000.0
LOG2_E = 1.4426950408889634

LANES = 128
MXU_COLS = 256
VMEM_BYTES = 64 * 1024 * 1024

N_REAL = BATCH * SEQ
N_ROWS = N_REAL + N_META
META_BLK = N_REAL // N_META
QK_PAD = MXU_COLS
LAT_W = Q_LORA + KV_LORA + LANES
MAIN_W = POOL_W + MLA_W + POOL_W + 2 * D_MODEL
LAT_SRC = Q_LORA + KV_LORA + ROPE

TM_MM = 656
TM_ROW = 400
TN = 1024
TQ = 512
TP = 256


def _vmem_limit(block_bytes, temp_bytes):
    need = 2 * sum(block_bytes) + temp_bytes
    assert need <= VMEM_BYTES - (4 << 20), need
    return int(need)


def _nbytes(shape, dtype):
    n = 1
    for s in shape:
        n *= s
    return n * jnp.dtype(dtype).itemsize


def _rmsnorm_kernel(h_ref, g_ref, o_ref):
    x = h_ref[...]
    ms = jnp.mean(x * x, axis=-1, keepdims=True)
    o_ref[...] = (x * lax.rsqrt(ms + NORM_EPS) * g_ref[...]).astype(o_ref.dtype)


def _rmsnorm(h, norm_g, layer):
    n_tiles = N_ROWS // TM_ROW
    blocks = [_nbytes((TM_ROW, D_MODEL), F32), _nbytes((TM_ROW, D_MODEL), BF16)]
    return pl.pallas_call(
        _rmsnorm_kernel,
        out_shape=jax.ShapeDtypeStruct((N_ROWS, D_MODEL), BF16),
        grid=(n_tiles,),
        in_specs=[
            pl.BlockSpec((TM_ROW, D_MODEL), lambda i: (i, 0)),
            pl.BlockSpec((None, 1, D_MODEL), lambda i: (layer, 0, 0)),
        ],
        out_specs=pl.BlockSpec((TM_ROW, D_MODEL), lambda i: (i, 0)),
        compiler_params=pltpu.CompilerParams(
            dimension_semantics=("parallel",),
            vmem_limit_bytes=_vmem_limit(blocks, _nbytes((TM_ROW, D_MODEL), F32))),
        name="rmsnorm",
    )(h, norm_g)


_RAW_TILES = POOL_W // TN
_SILU_TILES = (MLA_W + POOL_W) // TN


def _inproj_kernel(x_ref, w_ref, o_ref):
    j = pl.program_id(1)
    acc = jnp.dot(x_ref[...], w_ref[...], preferred_element_type=F32)
    sig = jax.nn.sigmoid(acc)
    out = jnp.where(j < _RAW_TILES, acc,
                    jnp.where(j < _RAW_TILES + _SILU_TILES, acc * sig, sig))
    o_ref[...] = out.astype(o_ref.dtype)


def _inproj(hn, w_main, layer):
    blocks = [_nbytes((TM_MM, D_MODEL), BF16), _nbytes((D_MODEL, TN), BF16),
              _nbytes((TM_MM, TN), BF16)]
    return pl.pallas_call(
        _inproj_kernel,
        out_shape=jax.ShapeDtypeStruct((N_ROWS, MAIN_W), BF16),
        grid=(N_ROWS // TM_MM, MAIN_W // TN),
        in_specs=[
            pl.BlockSpec((TM_MM, D_MODEL), lambda i, j: (i, 0)),
            pl.BlockSpec((None, D_MODEL, TN), lambda i, j: (layer, 0, j)),
        ],
        out_specs=pl.BlockSpec((TM_MM, TN), lambda i, j: (i, j)),
        compiler_params=pltpu.CompilerParams(
            dimension_semantics=("parallel", "parallel"),
            vmem_limit_bytes=_vmem_limit(blocks, 3 * _nbytes((TM_MM, TN), F32))),
        name="inproj",
    )(hn, w_main)


def _latent_kernel(x_ref, w_ref, o_ref):
    o_ref[...] = jnp.dot(x_ref[...], w_ref[...], preferred_element_type=F32)


def _latent(hn, w_lat, layer):
    blocks = [_nbytes((TM_ROW, D_MODEL), BF16), _nbytes((D_MODEL, LAT_W), BF16),
              _nbytes((TM_ROW, LAT_W), F32)]
    return pl.pallas_call(
        _latent_kernel,
        out_shape=jax.ShapeDtypeStruct((N_ROWS, LAT_W), F32),
        grid=(N_ROWS // TM_ROW,),
        in_specs=[
            pl.BlockSpec((TM_ROW, D_MODEL), lambda i: (i, 0)),
            pl.BlockSpec((None, D_MODEL, LAT_W), lambda i: (layer, 0, 0)),
        ],
        out_specs=pl.BlockSpec((TM_ROW, LAT_W), lambda i: (i, 0)),
        compiler_params=pltpu.CompilerParams(
            dimension_semantics=("parallel",),
            vmem_limit_bytes=_vmem_limit(blocks, _nbytes((TM_ROW, LAT_W), F32))),
        name="latent",
    )(hn, w_lat)


def _rope(x, cos, sin_signed):
    partner = pltpu.roll(x, ROPE // 2, axis=1) + pltpu.roll(x, LANES - ROPE // 2, axis=1)
    return x * cos + partner * sin_signed


def _mla_prep_kernel(lat_ref, gql_ref, gkvl_ref, wuq_ref, wukv_ref, gq_ref, gk_ref,
                     cos_ref, sin_ref, q_ref, k_ref, v_ref):
    cos = cos_ref[...]
    sin = sin_ref[...]
    cq = lat_ref[:, 0:Q_LORA]
    ckv = lat_ref[:, Q_LORA:Q_LORA + KV_LORA]
    kr = lat_ref[:, Q_LORA + KV_LORA:LAT_W]
    cqn = (cq * lax.rsqrt(jnp.mean(cq * cq, axis=-1, keepdims=True) + NORM_EPS)
           * gql_ref[...]).astype(BF16)
    ckvn = (ckv * lax.rsqrt(jnp.mean(ckv * ckv, axis=-1, keepdims=True) + NORM_EPS)
            * gkvl_ref[...]).astype(BF16)
    gq = gq_ref[...]
    gk = gk_ref[...]
    kr_ssq = jnp.sum(kr * kr, axis=-1, keepdims=True)
    kr_roped = _rope(kr * gk[:, NOPE:QK_PAD], cos, sin)
    for h in range(HEADS):
        c0 = h * QK_PAD
        qh = jnp.dot(cqn, wuq_ref[:, c0:c0 + QK_PAD], preferred_element_type=F32)
        rq = lax.rsqrt(jnp.sum(qh * qh, axis=-1, keepdims=True) * (1.0 / QK) + NORM_EPS)
        qn = qh * rq * gq
        q_ref[:, c0:c0 + NOPE] = qn[:, 0:NOPE].astype(BF16)
        q_ref[:, c0 + NOPE:c0 + QK_PAD] = _rope(qn[:, NOPE:QK_PAD], cos, sin).astype(BF16)
        kvh = jnp.dot(ckvn, wukv_ref[:, c0:c0 + QK_PAD], preferred_element_type=F32)
        kn = kvh[:, 0:NOPE]
        rk = lax.rsqrt((jnp.sum(kn * kn, axis=-1, keepdims=True) + kr_ssq) * (1.0 / QK)
                       + NORM_EPS)
        k_ref[:, c0:c0 + NOPE] = (kn * rk * gk[:, 0:NOPE]).astype(BF16)
        k_ref[:, c0 + NOPE:c0 + QK_PAD] = (kr_roped * rk).astype(BF16)
        v_ref[:, h * V_DIM:(h + 1) * V_DIM] = kvh[:, NOPE:NOPE + V_DIM].astype(BF16)


def _mla_prep(lat, gql, gkvl, wuq, wukv, gq, gk, cos, sin, layer):
    blocks = [_nbytes((TM_ROW, LAT_W), F32), _nbytes((Q_LORA, HEADS * QK_PAD), BF16),
              _nbytes((KV_LORA, HEADS * QK_PAD), BF16),
              2 * _nbytes((TM_ROW, HEADS * QK_PAD), BF16), _nbytes((TM_ROW, MLA_W), BF16),
              2 * _nbytes((TM_ROW, LANES), F32)]
    row = lambda i: (i, 0)
    par = lambda i: (layer, 0, 0)
    return pl.pallas_call(
        _mla_prep_kernel,
        out_shape=(jax.ShapeDtypeStruct((N_ROWS, HEADS * QK_PAD), BF16),
                   jax.ShapeDtypeStruct((N_ROWS, HEADS * QK_PAD), BF16),
                   jax.ShapeDtypeStruct((N_ROWS, MLA_W), BF16)),
        grid=(N_ROWS // TM_ROW,),
        in_specs=[
            pl.BlockSpec((TM_ROW, LAT_W), row),
            pl.BlockSpec((None, 1, Q_LORA), par),
            pl.BlockSpec((None, 1, KV_LORA), par),
            pl.BlockSpec((None, Q_LORA, HEADS * QK_PAD), par),
            pl.BlockSpec((None, KV_LORA, HEADS * QK_PAD), par),
            pl.BlockSpec((None, 1, QK_PAD), par),
            pl.BlockSpec((None, 1, QK_PAD), par),
            pl.BlockSpec((TM_ROW, LANES), row),
            pl.BlockSpec((TM_ROW, LANES), row),
        ],
        out_specs=(pl.BlockSpec((TM_ROW, HEADS * QK_PAD), row),
                   pl.BlockSpec((TM_ROW, HEADS * QK_PAD), row),
                   pl.BlockSpec((TM_ROW, MLA_W), row)),
        compiler_params=pltpu.CompilerParams(
            dimension_semantics=("parallel",),
            vmem_limit_bytes=_vmem_limit(
                blocks, _nbytes((TM_ROW, Q_LORA + KV_LORA), BF16)
                + 6 * _nbytes((TM_ROW, QK_PAD), F32))),
        name="mla_prep",
    )(lat, gql, gkvl, wuq, wukv, gq, gk, cos, sin)


_NT = (((1,), (1,)), ((), ()))
_MASKED = -1e30


def _row_max(x):
    return jnp.max(x, axis=-1, keepdims=True)


def _row_sum(x):
    return jnp.sum(x, axis=-1, keepdims=True)


def _tile_update(s, v, state, diagonal, meta=None):
    if not diagonal:
        row_max = _row_max(s)
        if meta is not None:
            row_max = jnp.maximum(row_max, _row_max(meta[0]))
        m_new = row_max if state is None else jnp.maximum(state[0], row_max)
        p = jnp.exp2(s - m_new)
        row_sum = _row_sum(p)
        p = p.astype(BF16)
        if meta is not None:
            p_meta = jnp.exp2(meta[0] - m_new)
            row_sum = row_sum + _row_sum(p_meta)
            p_meta = p_meta.astype(BF16)
    else:
        n_blk = s.shape[0] // LANES
        q_chunk = lax.broadcasted_iota(jnp.int32, (LANES, LANES), 0) // CHUNK
        k_chunk = lax.broadcasted_iota(jnp.int32, (LANES, LANES), 1) // CHUNK
        visible = k_chunk <= q_chunk
        m_parts, p_parts, pm_parts, sum_parts = [], [], [], []
        for rb in range(n_blk):
            rows = slice(rb * LANES, (rb + 1) * LANES)
            blocks = [jnp.where(visible, s[rows, rb * LANES:(rb + 1) * LANES], _MASKED)]
            if rb > 0:
                blocks.insert(0, s[rows, 0:rb * LANES])
            if meta is not None:
                blocks.insert(0, meta[0][rows])
            s_vis = jnp.concatenate(blocks, axis=1) if len(blocks) > 1 else blocks[0]
            m_rb = _row_max(s_vis)
            if state is not None:
                m_rb = jnp.maximum(state[0][rows], m_rb)
            p_rb = jnp.exp2(s_vis - m_rb)
            sum_parts.append(_row_sum(p_rb))
            p_rb = p_rb.astype(BF16)
            if meta is not None:
                pm_parts.append(p_rb[:, 0:LANES])
                p_rb = p_rb[:, LANES:]
            if rb < n_blk - 1:
                hidden = jnp.zeros((LANES, (n_blk - 1 - rb) * LANES), BF16)
                p_rb = jnp.concatenate([p_rb, hidden], axis=1)
            m_parts.append(m_rb)
            p_parts.append(p_rb)
        m_new = jnp.concatenate(m_parts, axis=0)
        p = jnp.concatenate(p_parts, axis=0)
        row_sum = jnp.concatenate(sum_parts, axis=0)
        if meta is not None:
            p_meta = jnp.concatenate(pm_parts, axis=0)
    pv = jnp.dot(p, v, preferred_element_type=F32)
    if meta is not None:
        pv = pv + jnp.dot(p_meta, meta[1], preferred_element_type=F32)
    if state is None:
        return m_new, row_sum, pv
    alpha = jnp.exp2(state[0] - m_new)
    return m_new, alpha * state[1] + row_sum, alpha * state[2] + pv


def _attn_kernel(q_ref, k_ref, v_ref, km_ref, vm_ref, g_ref, o_ref, *, real_keys):
    pad = LANES - N_META
    km = jnp.concatenate([km_ref[...], jnp.zeros((pad, QK_PAD), BF16)], axis=0)
    vm = jnp.concatenate([vm_ref[...], jnp.zeros((pad, V_DIM), BF16)], axis=0)

    def meta_scores(q):
        s = lax.dot_general(q, km, _NT, preferred_element_type=F32)
        return jnp.where(lax.broadcasted_iota(jnp.int32, s.shape, 1) < N_META, s, _MASKED)

    def finish(rows, l, acc):
        o_ref[rows, :] = (acc / l * g_ref[rows, :].astype(F32)).astype(o_ref.dtype)

    def chain(tile):
        rows = slice(tile * TQ, (tile + 1) * TQ)
        q = q_ref[rows, :]

        def scores(j):
            return lax.dot_general(q, k_ref[j * TQ:(j + 1) * TQ, :], _NT,
                                   preferred_element_type=F32)

        meta = (meta_scores(q), vm)
        state = None
        s_next = scores(0)
        for j in range(tile + 1):
            if j > 0:
                yield
            s = s_next
            if j < tile:
                s_next = scores(j + 1)
            state = _tile_update(s, v_ref[j * TQ:(j + 1) * TQ, :], state, diagonal=j == tile,
                                 meta=meta if j == 0 else None)
        finish(rows, state[1], state[2])

    def pair(short):
        lengths = [SEQ // TQ - short, short + 1]
        chains = [chain(SEQ // TQ - 1 - short), chain(short)]
        done = [0, 0]
        while chains[0] is not None or chains[1] is not None:
            live = [c for c in (0, 1) if chains[c] is not None]
            c = min(live, key=lambda c: (done[c] + 0.5) / lengths[c])
            try:
                next(chains[c])
                done[c] += 1
            except StopIteration:
                chains[c] = None

    if real_keys:
        step = pl.program_id(2)
        for short in range(SEQ // TQ // 2):
            pl.when(step == short)(functools.partial(pair, short))
    else:
        s = meta_scores(q_ref[...])
        m = _row_max(s)
        p = jnp.exp2(s - m)
        acc = jnp.dot(p.astype(BF16), vm, preferred_element_type=F32)
        finish(slice(None), _row_sum(p), acc)


_GATE_A_BLK = POOL_W // V_DIM


def _attention(q, k, v, proj, gated_prev=None):
    if gated_prev is None:
        blocks = [2 * _nbytes((SEQ, QK_PAD), BF16), 3 * _nbytes((SEQ, V_DIM), BF16),
                  _nbytes((N_META, QK_PAD + V_DIM), BF16)]
        head = lambda b, h, step: (b, h)
        return pl.pallas_call(
            functools.partial(_attn_kernel, real_keys=True),
            out_shape=jax.ShapeDtypeStruct((N_ROWS, MLA_W), BF16),
            grid=(BATCH, HEADS, SEQ // TQ // 2),
            in_specs=[
                pl.BlockSpec((SEQ, QK_PAD), head),
                pl.BlockSpec((SEQ, QK_PAD), head),
                pl.BlockSpec((SEQ, V_DIM), head),
                pl.BlockSpec((N_META, QK_PAD), lambda b, h, step: (META_BLK, h)),
                pl.BlockSpec((N_META, V_DIM), lambda b, h, step: (META_BLK, h)),
                pl.BlockSpec((SEQ, V_DIM), lambda b, h, step: (b, _GATE_A_BLK + h)),
            ],
            out_specs=pl.BlockSpec((SEQ, V_DIM), head),
            compiler_params=pltpu.CompilerParams(
                dimension_semantics=("parallel", "parallel", "arbitrary"),
                vmem_limit_bytes=_vmem_limit(blocks, 12 * _nbytes((TQ, TQ), F32))),
            name="attention",
        )(q, k, v, k, v, proj)
    blocks = [_nbytes((N_META, QK_PAD), BF16), _nbytes((N_META, QK_PAD + V_DIM), BF16),
              2 * _nbytes((N_META, V_DIM), BF16)]
    meta = lambda h: (META_BLK, h)
    return pl.pallas_call(
        lambda q_ref, km_ref, vm_ref, g_ref, prev_ref, o_ref: _attn_kernel(
            q_ref, None, None, km_ref, vm_ref, g_ref, o_ref, real_keys=False),
        out_shape=jax.ShapeDtypeStruct((N_ROWS, MLA_W), BF16),
        grid=(HEADS,),
        in_specs=[
            pl.BlockSpec((N_META, QK_PAD), meta),
            pl.BlockSpec((N_META, QK_PAD), meta),
            pl.BlockSpec((N_META, V_DIM), meta),
            pl.BlockSpec((N_META, V_DIM), lambda h: (META_BLK, _GATE_A_BLK + h)),
            pl.BlockSpec(memory_space=pl.ANY),
        ],
        out_specs=pl.BlockSpec((N_META, V_DIM), meta),
        input_output_aliases={4: 0},
        compiler_params=pltpu.CompilerParams(
            dimension_semantics=("parallel",),
            vmem_limit_bytes=_vmem_limit(blocks, 8 * _nbytes((LANES, QK_PAD), F32))),
        name="attention_meta",
    )(q, k, v, proj, gated_prev)


def _pool_kernel(u_ref, halo_ref, bm_ref, bh_ref, invc_ref, wp_ref, sc_ref, g_ref, o_ref,
                 *, has_main):
    pad = LANES - N_META
    for g in range(GROUPS):
        cs = slice(g * GROUP_W, (g + 1) * GROUP_W)
        halo = jnp.concatenate([halo_ref[:, cs], jnp.zeros((pad, GROUP_W), BF16)], axis=0)
        win = jnp.dot(bh_ref[g], halo, preferred_element_type=F32)
        u = u_ref[:, cs]
        if has_main:
            win = win + jnp.dot(bm_ref[g], u, preferred_element_type=F32)
        diff = win * invc_ref[g][:, 0:1] - u.astype(F32)
        mixed = jnp.dot(diff.astype(BF16), wp_ref[g], preferred_element_type=F32)
        out = mixed * sc_ref[:, cs] * g_ref[:, cs].astype(F32)
        o_ref[:, cs] = out.astype(o_ref.dtype)


_GATE_B_BLK = (POOL_W + MLA_W) // POOL_W


def _pool_bands(rows, meta):
    t = jnp.arange(rows)[:, None]
    s = jnp.arange(rows)[None, :]
    s_h = jnp.arange(LANES)[None, :]
    main, halo, invc = [], [], []
    for w in WINDOWS:
        main.append(((t - s >= 0) & (t - s < w)).astype(BF16))
        if meta:
            halo.append(((t - s_h >= 0) & (t - s_h < w) & (s_h < N_META)).astype(BF16))
            cnt = jnp.minimum(t + 1, w).astype(F32)
        else:
            dist = t + N_META - s_h
            halo.append(((dist < w) & (s_h < N_META)).astype(BF16))
            cnt = jnp.full((rows, 1), w, F32)
        invc.append(jnp.broadcast_to(1.0 / cnt, (rows, LANES)))
    return jnp.stack(main), jnp.stack(halo), jnp.stack(invc)


def _pool(proj, w_pool, pool_scale, layer, gated_prev=None):
    meta = gated_prev is not None
    rows = N_META if meta else TP
    bm, bh, invc = _pool_bands(rows, meta)
    nt = SEQ // TP
    if meta:
        grid = (1,)
        tile = lambda t: (META_BLK, 0)
        halo = tile
        gate = lambda t: (META_BLK, _GATE_B_BLK)
        const3 = lambda t: (0, 0, 0)
        wmap = lambda t: (layer, 0, 0, 0)
        smap = lambda t: (layer, 0, 0)
    else:
        grid = (BATCH, nt)
        tile = lambda b, t: (b * nt + t, 0)
        halo = lambda b, t: (jnp.where(t == 0, META_BLK,
                                       (b * SEQ + t * TP) // N_META - 1), 0)
        gate = lambda b, t: (b * nt + t, _GATE_B_BLK)
        const3 = lambda b, t: (0, 0, 0)
        wmap = lambda b, t: (layer, 0, 0, 0)
        smap = lambda b, t: (layer, 0, 0)
    blocks = [3 * _nbytes((rows, POOL_W), BF16), _nbytes((N_META, POOL_W), BF16),
              _nbytes((GROUPS, rows, rows + LANES), BF16), _nbytes((GROUPS, rows, LANES), F32),
              _nbytes((GROUPS, GROUP_W, GROUP_W), BF16), _nbytes((1, POOL_W), F32)]
    in_specs = [
        pl.BlockSpec((rows, POOL_W), tile),
        pl.BlockSpec((N_META, POOL_W), halo),
        pl.BlockSpec((GROUPS, rows, rows), const3),
        pl.BlockSpec((GROUPS, rows, LANES), const3),
        pl.BlockSpec((GROUPS, rows, LANES), const3),
        pl.BlockSpec((None, GROUPS, GROUP_W, GROUP_W), wmap),
        pl.BlockSpec((None, 1, POOL_W), smap),
        pl.BlockSpec((rows, POOL_W), gate),
    ]
    args = [proj, proj, bm, bh, invc, w_pool, pool_scale, proj]
    kern = functools.partial(_pool_kernel, has_main=not meta)
    aliases = {}
    if meta:
        in_specs.append(pl.BlockSpec(memory_space=pl.ANY))
        args.append(gated_prev)
        aliases = {8: 0}
        inner = kern
        kern = lambda *refs: inner(*refs[:8], refs[9])
    return pl.pallas_call(
        kern,
        out_shape=jax.ShapeDtypeStruct((N_ROWS, POOL_W), BF16),
        grid=grid,
        in_specs=in_specs,
        out_specs=pl.BlockSpec((rows, POOL_W), tile),
        input_output_aliases=aliases,
        compiler_params=pltpu.CompilerParams(
            dimension_semantics=("parallel",) * len(grid),
            vmem_limit_bytes=_vmem_limit(blocks, 6 * _nbytes((max(rows, LANES), GROUP_W), F32))),
        name="pool_meta" if meta else "pool",
    )(*args)


def _merge_kernel(ga_ref, gb_ref, wa_ref, wb_ref, sa_ref, sb_ref, o_ref):
    a = jnp.dot(ga_ref[...], wa_ref[...], preferred_element_type=F32)
    b = jnp.dot(gb_ref[...], wb_ref[...], preferred_element_type=F32)
    o_ref[...] = (sa_ref[...].astype(F32) * a + sb_ref[...].astype(F32) * b).astype(o_ref.dtype)


_MERGE_A_BLK = (POOL_W + MLA_W + POOL_W) // TN
_MERGE_B_BLK = _MERGE_A_BLK + D_MODEL // TN


def _merge(ga, gb, w_a, w_b, proj, layer):
    blocks = [2 * _nbytes((TM_MM, MLA_W), BF16), 2 * _nbytes((MLA_W, TN), BF16),
              3 * _nbytes((TM_MM, TN), BF16)]
    return pl.pallas_call(
        _merge_kernel,
        out_shape=jax.ShapeDtypeStruct((N_ROWS, D_MODEL), BF16),
        grid=(N_ROWS // TM_MM, D_MODEL // TN),
        in_specs=[
            pl.BlockSpec((TM_MM, MLA_W), lambda i, j: (i, 0)),
            pl.BlockSpec((TM_MM, POOL_W), lambda i, j: (i, 0)),
            pl.BlockSpec((None, MLA_W, TN), lambda i, j: (layer, 0, j)),
            pl.BlockSpec((None, POOL_W, TN), lambda i, j: (layer, 0, j)),
            pl.BlockSpec((TM_MM, TN), lambda i, j: (i, _MERGE_A_BLK + j)),
            pl.BlockSpec((TM_MM, TN), lambda i, j: (i, _MERGE_B_BLK + j)),
        ],
        out_specs=pl.BlockSpec((TM_MM, TN), lambda i, j: (i, j)),
        compiler_params=pltpu.CompilerParams(
            dimension_semantics=("parallel", "parallel"),
            vmem_limit_bytes=_vmem_limit(blocks, 3 * _nbytes((TM_MM, TN), F32))),
        name="merge",
    )(ga, gb, w_a, w_b, proj, proj)


def _outproj_kernel(m_ref, w_ref, h_ref, o_ref):
    o_ref[...] = h_ref[...] + jnp.dot(m_ref[...], w_ref[...], preferred_element_type=F32)


def _outproj(merged, w_out, h, layer, last):
    blocks = [_nbytes((TM_MM, D_MODEL), BF16), _nbytes((D_MODEL, TN), BF16),
              2 * _nbytes((TM_MM, TN), F32)]
    out_rows = N_REAL if last else N_ROWS
    return pl.pallas_call(
        _outproj_kernel,
        out_shape=jax.ShapeDtypeStruct((out_rows, D_MODEL), F32),
        grid=(N_ROWS // TM_MM, D_MODEL // TN),
        in_specs=[
            pl.BlockSpec((TM_MM, D_MODEL), lambda i, j: (i, 0)),
            pl.BlockSpec((None, D_MODEL, TN), lambda i, j: (layer, 0, j)),
            pl.BlockSpec((TM_MM, TN), lambda i, j: (i, j)),
        ],
        out_specs=pl.BlockSpec((TM_MM, TN), lambda i, j: (i, j)),
        input_output_aliases={} if last else {2: 0},
        compiler_params=pltpu.CompilerParams(
            dimension_semantics=("parallel", "parallel"),
            vmem_limit_bytes=_vmem_limit(blocks, 3 * _nbytes((TM_MM, TN), F32))),
        name="outproj",
    )(merged, w_out, h)


_CAST_ROWS = 1024
_SHIFT = LAT_SRC % LANES


def _cast_main_kernel(a_ref, b_ref, c_ref, o_ref):
    window = jnp.concatenate([a_ref[...], b_ref[...], c_ref[...]], axis=1)
    o_ref[...] = window[:, _SHIFT:_SHIFT + TN].astype(o_ref.dtype)


def _cast_main_weights(w_in):
    base = LAT_SRC - _SHIFT
    half = TN // 2
    assert base % half == 0 and (base + TN) % LANES == 0
    blocks = [2 * _nbytes((_CAST_ROWS, half), F32), _nbytes((_CAST_ROWS, LANES), F32),
              _nbytes((_CAST_ROWS, TN), BF16)]
    return pl.pallas_call(
        _cast_main_kernel,
        out_shape=jax.ShapeDtypeStruct((DEPTH, D_MODEL, MAIN_W), BF16),
        grid=(DEPTH, D_MODEL // _CAST_ROWS, MAIN_W // TN),
        in_specs=[
            pl.BlockSpec((None, _CAST_ROWS, half), lambda l, r, j: (l, r, base // half + 2 * j)),
            pl.BlockSpec((None, _CAST_ROWS, half),
                         lambda l, r, j: (l, r, base // half + 1 + 2 * j)),
            pl.BlockSpec((None, _CAST_ROWS, LANES),
                         lambda l, r, j: (l, r, (base + TN) // LANES + (TN // LANES) * j)),
        ],
        out_specs=pl.BlockSpec((None, _CAST_ROWS, TN), lambda l, r, j: (l, r, j)),
        compiler_params=pltpu.CompilerParams(
            dimension_semantics=("parallel", "parallel", "parallel"),
            vmem_limit_bytes=_vmem_limit(blocks, 2 * _nbytes((_CAST_ROWS, TN + LANES), F32))),
        name="cast_main",
    )(w_in, w_in, w_in)
def _rope_tables():
    pos = jnp.concatenate([
        jnp.tile(jnp.arange(SEQ, dtype=F32) + N_META, BATCH),
        jnp.arange(N_META, dtype=F32)])
    inv = 1.0 / (ROPE_THETA ** (jnp.arange(0, ROPE, 2, dtype=F32) / ROPE))
    ang = pos[:, None] * inv[None, :]
    zeros = jnp.zeros((N_ROWS, LANES - ROPE), F32)
    cos = jnp.concatenate([jnp.cos(ang), jnp.cos(ang), zeros], axis=1)
    sin = jnp.concatenate([-jnp.sin(ang), jnp.sin(ang), zeros], axis=1)
    return cos, sin


def _pad_heads(w, width):
    lead = w.shape[:-1]
    w = w.reshape(lead + (HEADS, width))
    w = jnp.pad(w, [(0, 0)] * len(lead) + [(0, 0), (0, QK_PAD - width)])
    return w.reshape(lead + (HEADS * QK_PAD,))


def kernel(x, meta_tokens, norm_g, w_in, q_lora_g, kv_lora_g, w_uq, w_ukv, q_head_g, k_head_g,
           w_pool, pool_scale, w_branch_a, w_branch_b, w_out):
    assert x.shape == (BATCH, SEQ, D_MODEL) and x.dtype == F32
    assert w_in.shape == (DEPTH, D_MODEL, LAT_SRC + MAIN_W)

    w_lat = jnp.pad(w_in[:, :, :LAT_SRC], ((0, 0), (0, 0), (0, LAT_W - LAT_SRC))).astype(BF16)
    w_main = _cast_main_weights(w_in)
    w_uq_p = _pad_heads(w_uq, QK).astype(BF16)
    w_ukv_b = w_ukv.astype(BF16)
    w_pool_b = w_pool.astype(BF16)
    w_a = w_branch_a.astype(BF16)
    w_b = w_branch_b.astype(BF16)
    w_o = w_out.astype(BF16)
    head_pad = ((0, 0), (0, QK_PAD - QK))
    gq = (jnp.pad(q_head_g, head_pad) * (QK ** -0.5 * LOG2_E)).reshape(DEPTH, 1, QK_PAD)
    gk = jnp.pad(k_head_g, head_pad).reshape(DEPTH, 1, QK_PAD)
    norm_g3 = norm_g.reshape(DEPTH, 1, D_MODEL)
    gql = q_lora_g.reshape(DEPTH, 1, Q_LORA)
    gkvl = kv_lora_g.reshape(DEPTH, 1, KV_LORA)
    pscale = pool_scale.reshape(DEPTH, 1, POOL_W)
    cos, sin = _rope_tables()

    h = jnp.concatenate([x.reshape(N_REAL, D_MODEL), meta_tokens.astype(F32)], axis=0)
    for layer in range(DEPTH):
        hn = _rmsnorm(h, norm_g3, layer)
        proj = _inproj(hn, w_main, layer)
        lat = _latent(hn, w_lat, layer)
        q, k, v = _mla_prep(lat, gql, gkvl, w_uq_p, w_ukv_b, gq, gk, cos, sin, layer)
        ga = _attention(q, k, v, proj)
        ga = _attention(q, k, v, proj, gated_prev=ga)
        gb = _pool(proj, w_pool_b, pscale, layer)
        gb = _pool(proj, w_pool_b, pscale, layer, gated_prev=gb)
        merged = _merge(ga, gb, w_a, w_b, proj, layer)
        h = _outproj(merged, w_o, h, layer, last=layer == DEPTH - 1)
    return h.reshape(BATCH, SEQ, D_MODEL)
```

```python
import functools

import jax
import jax.numpy as jnp
from jax import lax
from jax.experimental import pallas as pl
from jax.experimental.pallas import tpu as pltpu

F32 = jnp.float32
BF16 = jnp.bfloat16

D_MODEL = 4096
BATCH = 4
SEQ = 4096
DEPTH = 4
CHUNK = 64
N_META = 16
HEADS = 16
NOPE = 128
ROPE = 64
QK = NOPE + ROPE
V_DIM = 128
MLA_W = HEADS * V_DIM
Q_LORA = 1024
KV_LORA = 512
POOL_W = 2048
WINDOWS = (2, 4, 8, 16)
GROUPS = len(WINDOWS)
GROUP_W = POOL_W // GROUPS
NORM_EPS = 1e-6
ROPE_THETA = 10000.0
LOG2_E = 1.4426950408889634

LANES = 128
MXU_COLS = 256
VMEM_BYTES = 64 * 1024 * 1024

N_REAL = BATCH * SEQ
N_ROWS = N_REAL + N_META
META_BLK = N_REAL // N_META
QK_PAD = MXU_COLS
LAT_W = Q_LORA + KV_LORA + LANES
MAIN_W = POOL_W + MLA_W + POOL_W + 2 * D_MODEL
LAT_SRC = Q_LORA + KV_LORA + ROPE

TM_MM = 656
TM_ROW = 400
TN = 1024
TQ = 512
TP = 256


def _vmem_limit(block_bytes, temp_bytes):
    need = 2 * sum(block_bytes) + temp_bytes
    assert need <= VMEM_BYTES - (4 << 20), need
    return int(need)


def _nbytes(shape, dtype):
    n = 1
    for s in shape:
        n *= s
    return n * jnp.dtype(dtype).itemsize


def _residual_tile(h_refs, emit):
    if len(h_refs) == 1:
        emit(h_refs[0][...])
        return
    x_ref, meta_ref = h_refs
    last = pl.num_programs(0) - 1
    n_real = N_REAL % x_ref.shape[0]
    assert n_real + N_META == x_ref.shape[0]
    pl.when(pl.program_id(0) != last)(lambda: emit(x_ref[...]))
    pl.when(pl.program_id(0) == last)(
        lambda: emit(jnp.concatenate([x_ref[0:n_real, :], meta_ref[...]], axis=0)))


def _rmsnorm_kernel(*refs):
    *h_refs, g_ref, o_ref = refs

    def emit(x):
        ms = jnp.mean(x * x, axis=-1, keepdims=True)
        o_ref[...] = (x * lax.rsqrt(ms + NORM_EPS) * g_ref[...]).astype(o_ref.dtype)

    _residual_tile(h_refs, emit)


def _rmsnorm(h_parts, norm_g, layer):
    n_tiles = N_ROWS // TM_ROW
    blocks = [_nbytes((TM_ROW, D_MODEL), F32), _nbytes((TM_ROW, D_MODEL), BF16),
              _nbytes((N_META, D_MODEL), F32)]
    h_specs = [pl.BlockSpec((TM_ROW, D_MODEL), lambda i: (i, 0))]
    if len(h_parts) == 2:
        h_specs.append(pl.BlockSpec((N_META, D_MODEL), lambda i: (0, 0)))
    return pl.pallas_call(
        _rmsnorm_kernel,
        out_shape=jax.ShapeDtypeStruct((N_ROWS, D_MODEL), BF16),
        grid=(n_tiles,),
        in_specs=h_specs + [pl.BlockSpec((None, 1, D_MODEL), lambda i: (layer, 0, 0))],
        out_specs=pl.BlockSpec((TM_ROW, D_MODEL), lambda i: (i, 0)),
        compiler_params=pltpu.CompilerParams(
            dimension_semantics=("parallel",),
            vmem_limit_bytes=_vmem_limit(blocks, 2 * _nbytes((TM_ROW, D_MODEL), F32))),
        name="rmsnorm",
    )(*h_parts, norm_g)


_RAW_TILES = POOL_W // TN
_SILU_TILES = (MLA_W + POOL_W) // TN


def _inproj_kernel(x_ref, w_ref, o_ref):
    j = pl.program_id(1)
    acc = jnp.dot(x_ref[...], w_ref[...], preferred_element_type=F32)
    sig = jax.nn.sigmoid(acc)
    out = jnp.where(j < _RAW_TILES, acc,
                    jnp.where(j < _RAW_TILES + _SILU_TILES, acc * sig, sig))
    o_ref[...] = out.astype(o_ref.dtype)


def _inproj(hn, w_main, layer):
    blocks = [_nbytes((TM_MM, D_MODEL), BF16), _nbytes((D_MODEL, TN), BF16),
              _nbytes((TM_MM, TN), BF16)]
    return pl.pallas_call(
        _inproj_kernel,
        out_shape=jax.ShapeDtypeStruct((N_ROWS, MAIN_W), BF16),
        grid=(N_ROWS // TM_MM, MAIN_W // TN),
        in_specs=[
            pl.BlockSpec((TM_MM, D_MODEL), lambda i, j: (i, 0)),
            pl.BlockSpec((None, D_MODEL, TN), lambda i, j: (layer, 0, j)),
        ],
        out_specs=pl.BlockSpec((TM_MM, TN), lambda i, j: (i, j)),
        compiler_params=pltpu.CompilerParams(
            dimension_semantics=("parallel", "parallel"),
            vmem_limit_bytes=_vmem_limit(blocks, 3 * _nbytes((TM_MM, TN), F32))),
        name="inproj",
    )(hn, w_main)


def _latent_kernel(x_ref, w_ref, o_ref):
    o_ref[...] = jnp.dot(x_ref[...], w_ref[...], preferred_element_type=F32)


def _latent(hn, w_lat, layer):
    blocks = [_nbytes((TM_ROW, D_MODEL), BF16), _nbytes((D_MODEL, LAT_W), BF16),
              _nbytes((TM_ROW, LAT_W), F32)]
    return pl.pallas_call(
        _latent_kernel,
        out_shape=jax.ShapeDtypeStruct((N_ROWS, LAT_W), F32),
        grid=(N_ROWS // TM_ROW,),
        in_specs=[
            pl.BlockSpec((TM_ROW, D_MODEL), lambda i: (i, 0)),
            pl.BlockSpec((None, D_MODEL, LAT_W), lambda i: (layer, 0, 0)),
        ],
        out_specs=pl.BlockSpec((TM_ROW, LAT_W), lambda i: (i, 0)),
        compiler_params=pltpu.CompilerParams(
            dimension_semantics=("parallel",),
            vmem_limit_bytes=_vmem_limit(blocks, _nbytes((TM_ROW, LAT_W), F32))),
        name="latent",
    )(hn, w_lat)


def _rope(x, cos, sin_signed):
    partner = pltpu.roll(x, ROPE // 2, axis=1) + pltpu.roll(x, LANES - ROPE // 2, axis=1)
    return x * cos + partner * sin_signed


def _mla_prep_kernel(lat_ref, gql_ref, gkvl_ref, wuq_ref, wukv_ref, gq_ref, gk_ref,
                     cos_ref, sin_ref, q_ref, k_ref, v_ref):
    cos = cos_ref[...]
    sin = sin_ref[...]
    cq = lat_ref[:, 0:Q_LORA]
    ckv = lat_ref[:, Q_LORA:Q_LORA + KV_LORA]
    kr = lat_ref[:, Q_LORA + KV_LORA:LAT_W]
    cqn = (cq * lax.rsqrt(jnp.mean(cq * cq, axis=-1, keepdims=True) + NORM_EPS)
           * gql_ref[...]).astype(BF16)
    ckvn = (ckv * lax.rsqrt(jnp.mean(ckv * ckv, axis=-1, keepdims=True) + NORM_EPS)
            * gkvl_ref[...]).astype(BF16)
    gq = gq_ref[...]
    gk = gk_ref[...]
    kr_ssq = jnp.sum(kr * kr, axis=-1, keepdims=True)
    kr_roped = _rope(kr * gk[:, NOPE:QK_PAD], cos, sin)
    for h in range(HEADS):
        c0 = h * QK_PAD
        qh = jnp.dot(cqn, wuq_ref[:, c0:c0 + QK_PAD], preferred_element_type=F32)
        rq = lax.rsqrt(jnp.sum(qh * qh, axis=-1, keepdims=True) * (1.0 / QK) + NORM_EPS)
        qn = qh * rq * gq
        q_ref[:, c0:c0 + NOPE] = qn[:, 0:NOPE].astype(BF16)
        q_ref[:, c0 + NOPE:c0 + QK_PAD] = _rope(qn[:, NOPE:QK_PAD], cos, sin).astype(BF16)
        kvh = jnp.dot(ckvn, wukv_ref[:, c0:c0 + QK_PAD], preferred_element_type=F32)
        kn = kvh[:, 0:NOPE]
        rk = lax.rsqrt((jnp.sum(kn * kn, axis=-1, keepdims=True) + kr_ssq) * (1.0 / QK)
                       + NORM_EPS)
        k_ref[:, c0:c0 + NOPE] = (kn * rk * gk[:, 0:NOPE]).astype(BF16)
        k_ref[:, c0 + NOPE:c0 + QK_PAD] = (kr_roped * rk).astype(BF16)
        v_ref[:, h * V_DIM:(h + 1) * V_DIM] = kvh[:, NOPE:NOPE + V_DIM].astype(BF16)


def _mla_prep(lat, gql, gkvl, wuq, wukv, gq, gk, cos, sin, layer):
    blocks = [_nbytes((TM_ROW, LAT_W), F32), _nbytes((Q_LORA, HEADS * QK_PAD), BF16),
              _nbytes((KV_LORA, HEADS * QK_PAD), BF16),
              2 * _nbytes((TM_ROW, HEADS * QK_PAD), BF16), _nbytes((TM_ROW, MLA_W), BF16),
              2 * _nbytes((TM_ROW, LANES), F32)]
    row = lambda i: (i, 0)
    par = lambda i: (layer, 0, 0)
    return pl.pallas_call(
        _mla_prep_kernel,
        out_shape=(jax.ShapeDtypeStruct((N_ROWS, HEADS * QK_PAD), BF16),
                   jax.ShapeDtypeStruct((N_ROWS, HEADS * QK_PAD), BF16),
                   jax.ShapeDtypeStruct((N_ROWS, MLA_W), BF16)),
        grid=(N_ROWS // TM_ROW,),
        in_specs=[
            pl.BlockSpec((TM_ROW, LAT_W), row),
            pl.BlockSpec((None, 1, Q_LORA), par),
            pl.BlockSpec((None, 1, KV_LORA), par),
            pl.BlockSpec((None, Q_LORA, HEADS * QK_PAD), par),
            pl.BlockSpec((None, KV_LORA, HEADS * QK_PAD), par),
            pl.BlockSpec((None, 1, QK_PAD), par),
            pl.BlockSpec((None, 1, QK_PAD), par),
            pl.BlockSpec((TM_ROW, LANES), row),
            pl.BlockSpec((TM_ROW, LANES), row),
        ],
        out_specs=(pl.BlockSpec((TM_ROW, HEADS * QK_PAD), row),
                   pl.BlockSpec((TM_ROW, HEADS * QK_PAD), row),
                   pl.BlockSpec((TM_ROW, MLA_W), row)),
        compiler_params=pltpu.CompilerParams(
            dimension_semantics=("parallel",),
            vmem_limit_bytes=_vmem_limit(
                blocks, _nbytes((TM_ROW, Q_LORA + KV_LORA), BF16)
                + 6 * _nbytes((TM_ROW, QK_PAD), F32))),
        name="mla_prep",
    )(lat, gql, gkvl, wuq, wukv, gq, gk, cos, sin)


_NT = (((1,), (1,)), ((), ()))
_MASKED = -1e30


def _row_max(x):
    return jnp.max(x, axis=-1, keepdims=True)


def _row_sum(x):
    return jnp.sum(x, axis=-1, keepdims=True)


def _tile_update(s, v, state, meta=None):
    row_max = _row_max(s)
    if meta is not None:
        row_max = jnp.maximum(row_max, _row_max(meta[0]))
    m_new = row_max if state is None else jnp.maximum(state[0], row_max)
    p = jnp.exp2(s - m_new)
    row_sum = _row_sum(p)
    pv = jnp.dot(p.astype(BF16), v, preferred_element_type=F32)
    if meta is not None:
        p_meta = jnp.exp2(meta[0] - m_new)
        row_sum = row_sum + _row_sum(p_meta)
        pv = pv + jnp.dot(p_meta.astype(BF16), meta[1], preferred_element_type=F32)
    if state is None:
        return m_new, row_sum, pv
    alpha = jnp.exp2(state[0] - m_new)
    return m_new, alpha * state[1] + row_sum, alpha * state[2] + pv


def _causal_bias():
    q_chunk = jnp.arange(TQ)[:, None] // CHUNK
    k_chunk = jnp.arange(TQ)[None, :] // CHUNK
    return jnp.where(k_chunk <= q_chunk, 0.0, _MASKED).astype(F32)


def _attn_kernel(q_ref, k_ref, v_ref, km_ref, vm_ref, g_ref, bias_ref, o_ref, *, real_keys):
    pad = LANES - N_META
    km = jnp.concatenate([km_ref[...], jnp.zeros((pad, QK_PAD), BF16)], axis=0)
    vm = jnp.concatenate([vm_ref[...], jnp.zeros((pad, V_DIM), BF16)], axis=0)

    def meta_scores(q):
        s = lax.dot_general(q, km, _NT, preferred_element_type=F32)
        return jnp.where(lax.broadcasted_iota(jnp.int32, s.shape, 1) < N_META, s, _MASKED)

    def finish(rows, l, acc):
        o_ref[rows, :] = (acc / l * g_ref[rows, :].astype(F32)).astype(o_ref.dtype)

    def chain(tile):
        rows = slice(tile * TQ, (tile + 1) * TQ)
        q = q_ref[rows, :]
        order = list(range(tile))
        order.insert(tile // 2, tile)

        def scores(j):
            s = lax.dot_general(q, k_ref[j * TQ:(j + 1) * TQ, :], _NT,
                                preferred_element_type=F32)
            return s + bias_ref[...] if j == tile else s

        meta = (meta_scores(q), vm)
        state = None
        s_next = scores(order[0])
        for n, j in enumerate(order):
            if n > 0:
                yield
            s = s_next
            if n + 1 < len(order):
                s_next = scores(order[n + 1])
            state = _tile_update(s, v_ref[j * TQ:(j + 1) * TQ, :], state,
                                 meta=meta if n == 0 else None)
        finish(rows, state[1], state[2])

    def pair(short):
        lengths = [SEQ // TQ - short, short + 1]
        chains = [chain(SEQ // TQ - 1 - short), chain(short)]
        done = [0, 0]
        while chains[0] is not None or chains[1] is not None:
            live = [c for c in (0, 1) if chains[c] is not None]
            c = min(live, key=lambda c: (done[c] + 0.5) / lengths[c])
            try:
                next(chains[c])
                done[c] += 1
            except StopIteration:
                chains[c] = None

    if real_keys:
        step = pl.program_id(2)
        for short in range(SEQ // TQ // 2):
            pl.when(step == short)(functools.partial(pair, short))
    else:
        s = meta_scores(q_ref[...])
        m = _row_max(s)
        p = jnp.exp2(s - m)
        acc = jnp.dot(p.astype(BF16), vm, preferred_element_type=F32)
        finish(slice(None), _row_sum(p), acc)


_GATE_A_BLK = POOL_W // V_DIM


def _attention(q, k, v, proj, gated_prev=None):
    if gated_prev is None:
        blocks = [2 * _nbytes((SEQ, QK_PAD), BF16), 3 * _nbytes((SEQ, V_DIM), BF16),
                  _nbytes((N_META, QK_PAD + V_DIM), BF16), _nbytes((TQ, TQ), F32)]
        head = lambda b, h, step: (b, h)
        return pl.pallas_call(
            functools.partial(_attn_kernel, real_keys=True),
            out_shape=jax.ShapeDtypeStruct((N_ROWS, MLA_W), BF16),
            grid=(BATCH, HEADS, SEQ // TQ // 2),
            in_specs=[
                pl.BlockSpec((SEQ, QK_PAD), head),
                pl.BlockSpec((SEQ, QK_PAD), head),
                pl.BlockSpec((SEQ, V_DIM), head),
                pl.BlockSpec((N_META, QK_PAD), lambda b, h, step: (META_BLK, h)),
                pl.BlockSpec((N_META, V_DIM), lambda b, h, step: (META_BLK, h)),
                pl.BlockSpec((SEQ, V_DIM), lambda b, h, step: (b, _GATE_A_BLK + h)),
                pl.BlockSpec((TQ, TQ), lambda b, h, step: (0, 0)),
            ],
            out_specs=pl.BlockSpec((SEQ, V_DIM), head),
            compiler_params=pltpu.CompilerParams(
                dimension_semantics=("parallel", "parallel", "arbitrary"),
                vmem_limit_bytes=_vmem_limit(blocks, 12 * _nbytes((TQ, TQ), F32))),
            name="attention",
        )(q, k, v, k, v, proj, _causal_bias())
    blocks = [_nbytes((N_META, QK_PAD), BF16), _nbytes((N_META, QK_PAD + V_DIM), BF16),
              2 * _nbytes((N_META, V_DIM), BF16)]
    meta = lambda h: (META_BLK, h)
    return pl.pallas_call(
        lambda q_ref, km_ref, vm_ref, g_ref, prev_ref, o_ref: _attn_kernel(
            q_ref, None, None, km_ref, vm_ref, g_ref, None, o_ref, real_keys=False),
        out_shape=jax.ShapeDtypeStruct((N_ROWS, MLA_W), BF16),
        grid=(HEADS,),
        in_specs=[
            pl.BlockSpec((N_META, QK_PAD), meta),
            pl.BlockSpec((N_META, QK_PAD), meta),
            pl.BlockSpec((N_META, V_DIM), meta),
            pl.BlockSpec((N_META, V_DIM), lambda h: (META_BLK, _GATE_A_BLK + h)),
            pl.BlockSpec(memory_space=pl.ANY),
        ],
        out_specs=pl.BlockSpec((N_META, V_DIM), meta),
        input_output_aliases={4: 0},
        compiler_params=pltpu.CompilerParams(
            dimension_semantics=("parallel",),
            vmem_limit_bytes=_vmem_limit(blocks, 8 * _nbytes((LANES, QK_PAD), F32))),
        name="attention_meta",
    )(q, k, v, proj, gated_prev)


def _pool_kernel(u_ref, halo_ref, bm_ref, bh_ref, invc_ref, wp_ref, sc_ref, g_ref, o_ref,
                 *, has_main):
    pad = LANES - N_META
    for g in range(GROUPS):
        cs = slice(g * GROUP_W, (g + 1) * GROUP_W)
        halo = jnp.concatenate([halo_ref[:, cs], jnp.zeros((pad, GROUP_W), BF16)], axis=0)
        win = jnp.dot(bh_ref[g], halo, preferred_element_type=F32)
        u = u_ref[:, cs]
        if has_main:
            win = win + jnp.dot(bm_ref[g], u, preferred_element_type=F32)
        diff = win * invc_ref[g][:, 0:1] - u.astype(F32)
        mixed = jnp.dot(diff.astype(BF16), wp_ref[g], preferred_element_type=F32)
        out = mixed * sc_ref[:, cs] * g_ref[:, cs].astype(F32)
        o_ref[:, cs] = out.astype(o_ref.dtype)


_GATE_B_BLK = (POOL_W + MLA_W) // POOL_W


def _pool_bands(rows, meta):
    t = jnp.arange(rows)[:, None]
    s = jnp.arange(rows)[None, :]
    s_h = jnp.arange(LANES)[None, :]
    main, halo, invc = [], [], []
    for w in WINDOWS:
        main.append(((t - s >= 0) & (t - s < w)).astype(BF16))
        if meta:
            halo.append(((t - s_h >= 0) & (t - s_h < w) & (s_h < N_META)).astype(BF16))
            cnt = jnp.minimum(t + 1, w).astype(F32)
        else:
            dist = t + N_META - s_h
            halo.append(((dist < w) & (s_h < N_META)).astype(BF16))
            cnt = jnp.full((rows, 1), w, F32)
        invc.append(jnp.broadcast_to(1.0 / cnt, (rows, LANES)))
    return jnp.stack(main), jnp.stack(halo), jnp.stack(invc)


def _pool(proj, w_pool, pool_scale, layer, gated_prev=None):
    meta = gated_prev is not None
    rows = N_META if meta else TP
    bm, bh, invc = _pool_bands(rows, meta)
    nt = SEQ // TP
    if meta:
        grid = (1,)
        tile = lambda t: (META_BLK, 0)
        halo = tile
        gate = lambda t: (META_BLK, _GATE_B_BLK)
        const3 = lambda t: (0, 0, 0)
        wmap = lambda t: (layer, 0, 0, 0)
        smap = lambda t: (layer, 0, 0)
    else:
        grid = (BATCH, nt)
        tile = lambda b, t: (b * nt + t, 0)
        halo = lambda b, t: (jnp.where(t == 0, META_BLK,
                                       (b * SEQ + t * TP) // N_META - 1), 0)
        gate = lambda b, t: (b * nt + t, _GATE_B_BLK)
        const3 = lambda b, t: (0, 0, 0)
        wmap = lambda b, t: (layer, 0, 0, 0)
        smap = lambda b, t: (layer, 0, 0)
    blocks = [3 * _nbytes((rows, POOL_W), BF16), _nbytes((N_META, POOL_W), BF16),
              _nbytes((GROUPS, rows, rows + LANES), BF16), _nbytes((GROUPS, rows, LANES), F32),
              _nbytes((GROUPS, GROUP_W, GROUP_W), BF16), _nbytes((1, POOL_W), F32)]
    in_specs = [
        pl.BlockSpec((rows, POOL_W), tile),
        pl.BlockSpec((N_META, POOL_W), halo),
        pl.BlockSpec((GROUPS, rows, rows), const3),
        pl.BlockSpec((GROUPS, rows, LANES), const3),
        pl.BlockSpec((GROUPS, rows, LANES), const3),
        pl.BlockSpec((None, GROUPS, GROUP_W, GROUP_W), wmap),
        pl.BlockSpec((None, 1, POOL_W), smap),
        pl.BlockSpec((rows, POOL_W), gate),
    ]
    args = [proj, proj, bm, bh, invc, w_pool, pool_scale, proj]
    kern = functools.partial(_pool_kernel, has_main=not meta)
    aliases = {}
    if meta:
        in_specs.append(pl.BlockSpec(memory_space=pl.ANY))
        args.append(gated_prev)
        aliases = {8: 0}
        inner = kern
        kern = lambda *refs: inner(*refs[:8], refs[9])
    return pl.pallas_call(
        kern,
        out_shape=jax.ShapeDtypeStruct((N_ROWS, POOL_W), BF16),
        grid=grid,
        in_specs=in_specs,
        out_specs=pl.BlockSpec((rows, POOL_W), tile),
        input_output_aliases=aliases,
        compiler_params=pltpu.CompilerParams(
            dimension_semantics=("parallel",) * len(grid),
            vmem_limit_bytes=_vmem_limit(blocks, 6 * _nbytes((max(rows, LANES), GROUP_W), F32))),
        name="pool_meta" if meta else "pool",
    )(*args)


def _merge_kernel(ga_ref, gb_ref, wa_ref, wb_ref, sa_ref, sb_ref, o_ref):
    a = jnp.dot(ga_ref[...], wa_ref[...], preferred_element_type=F32)
    b = jnp.dot(gb_ref[...], wb_ref[...], preferred_element_type=F32)
    o_ref[...] = (sa_ref[...].astype(F32) * a + sb_ref[...].astype(F32) * b).astype(o_ref.dtype)


_MERGE_A_BLK = (POOL_W + MLA_W + POOL_W) // TN
_MERGE_B_BLK = _MERGE_A_BLK + D_MODEL // TN


def _merge(ga, gb, w_a, w_b, proj, layer):
    blocks = [2 * _nbytes((TM_MM, MLA_W), BF16), 2 * _nbytes((MLA_W, TN), BF16),
              3 * _nbytes((TM_MM, TN), BF16)]
    return pl.pallas_call(
        _merge_kernel,
        out_shape=jax.ShapeDtypeStruct((N_ROWS, D_MODEL), BF16),
        grid=(N_ROWS // TM_MM, D_MODEL // TN),
        in_specs=[
            pl.BlockSpec((TM_MM, MLA_W), lambda i, j: (i, 0)),
            pl.BlockSpec((TM_MM, POOL_W), lambda i, j: (i, 0)),
            pl.BlockSpec((None, MLA_W, TN), lambda i, j: (layer, 0, j)),
            pl.BlockSpec((None, POOL_W, TN), lambda i, j: (layer, 0, j)),
            pl.BlockSpec((TM_MM, TN), lambda i, j: (i, _MERGE_A_BLK + j)),
            pl.BlockSpec((TM_MM, TN), lambda i, j: (i, _MERGE_B_BLK + j)),
        ],
        out_specs=pl.BlockSpec((TM_MM, TN), lambda i, j: (i, j)),
        compiler_params=pltpu.CompilerParams(
            dimension_semantics=("parallel", "parallel"),
            vmem_limit_bytes=_vmem_limit(blocks, 3 * _nbytes((TM_MM, TN), F32))),
        name="merge",
    )(ga, gb, w_a, w_b, proj, proj)


def _outproj_kernel(m_ref, w_ref, *refs):
    *h_refs, o_ref = refs
    delta = jnp.dot(m_ref[...], w_ref[...], preferred_element_type=F32)

    def emit(h):
        o_ref[...] = h + delta

    _residual_tile(h_refs, emit)


def _outproj(merged, w_out, h_parts, layer):
    blocks = [_nbytes((TM_MM, D_MODEL), BF16), _nbytes((D_MODEL, TN), BF16),
              2 * _nbytes((TM_MM, TN), F32), _nbytes((N_META, TN), F32)]
    out_rows = N_REAL if layer == DEPTH - 1 else N_ROWS
    h_specs = [pl.BlockSpec((TM_MM, TN), lambda i, j: (i, j))]
    if len(h_parts) == 2:
        h_specs.append(pl.BlockSpec((N_META, TN), lambda i, j: (0, j)))
    in_place = len(h_parts) == 1 and out_rows == N_ROWS
    return pl.pallas_call(
        _outproj_kernel,
        out_shape=jax.ShapeDtypeStruct((out_rows, D_MODEL), F32),
        grid=(N_ROWS // TM_MM, D_MODEL // TN),
        in_specs=[
            pl.BlockSpec((TM_MM, D_MODEL), lambda i, j: (i, 0)),
            pl.BlockSpec((None, D_MODEL, TN), lambda i, j: (layer, 0, j)),
        ] + h_specs,
        out_specs=pl.BlockSpec((TM_MM, TN), lambda i, j: (i, j)),
        input_output_aliases={2: 0} if in_place else {},
        compiler_params=pltpu.CompilerParams(
            dimension_semantics=("parallel", "parallel"),
            vmem_limit_bytes=_vmem_limit(blocks, 4 * _nbytes((TM_MM, TN), F32))),
        name="outproj",
    )(merged, w_out, *h_parts)


_CAST_K = 1024
_SHIFT = LAT_SRC % LANES


def _cast_main_kernel(a_ref, b_ref, c_ref, o_ref):
    window = jnp.concatenate([a_ref[...], b_ref[...], c_ref[...]], axis=0)
    o_ref[...] = window[_SHIFT:_SHIFT + TN, :].T.astype(o_ref.dtype)


def _cast_latent_kernel(w_ref, o_ref):
    col = lax.broadcasted_iota(jnp.int32, w_ref.shape, 0)
    o_ref[...] = jnp.where(col < LAT_SRC, w_ref[...], 0.0).T.astype(o_ref.dtype)


def _cast_latent_weights(w_in_t):
    blocks = [_nbytes((LAT_W, _CAST_K), F32), _nbytes((_CAST_K, LAT_W), BF16)]
    return pl.pallas_call(
        _cast_latent_kernel,
        out_shape=jax.ShapeDtypeStruct((DEPTH, D_MODEL, LAT_W), BF16),
        grid=(DEPTH, D_MODEL // _CAST_K),
        in_specs=[pl.BlockSpec((None, LAT_W, _CAST_K), lambda l, r: (l, 0, r))],
        out_specs=pl.BlockSpec((None, _CAST_K, LAT_W), lambda l, r: (l, r, 0)),
        compiler_params=pltpu.CompilerParams(
            dimension_semantics=("parallel", "parallel"),
            vmem_limit_bytes=_vmem_limit(blocks, 3 * _nbytes((_CAST_K, LAT_W), F32))),
        name="cast_latent",
    )(w_in_t)


def _cast_main_weights(w_in_t):
    base = LAT_SRC - _SHIFT
    half = TN // 2
    assert base % half == 0 and (base + TN) % LANES == 0
    blocks = [2 * _nbytes((half, _CAST_K), F32), _nbytes((LANES, _CAST_K), F32),
              _nbytes((_CAST_K, TN), BF16)]
    return pl.pallas_call(
        _cast_main_kernel,
        out_shape=jax.ShapeDtypeStruct((DEPTH, D_MODEL, MAIN_W), BF16),
        grid=(DEPTH, D_MODEL // _CAST_K, MAIN_W // TN),
        in_specs=[
            pl.BlockSpec((None, half, _CAST_K), lambda l, r, j: (l, base // half + 2 * j, r)),
            pl.BlockSpec((None, half, _CAST_K),
                         lambda l, r, j: (l, base // half + 1 + 2 * j, r)),
            pl.BlockSpec((None, LANES, _CAST_K),
                         lambda l, r, j: (l, (base + TN) // LANES + (TN // LANES) * j, r)),
        ],
        out_specs=pl.BlockSpec((None, _CAST_K, TN), lambda l, r, j: (l, r, j)),
        compiler_params=pltpu.CompilerParams(
            dimension_semantics=("parallel", "parallel", "parallel"),
            vmem_limit_bytes=_vmem_limit(blocks, 3 * _nbytes((TN + LANES, _CAST_K), F32))),
        name="cast_main",
    )(w_in_t, w_in_t, w_in_t)


def _rope_tables():
    pos = jnp.concatenate([
        jnp.tile(jnp.arange(SEQ, dtype=F32) + N_META, BATCH),
        jnp.arange(N_META, dtype=F32)])
    inv = 1.0 / (ROPE_THETA ** (jnp.arange(0, ROPE, 2, dtype=F32) / ROPE))
    ang = pos[:, None] * inv[None, :]
    zeros = jnp.zeros((N_ROWS, LANES - ROPE), F32)
    cos = jnp.concatenate([jnp.cos(ang), jnp.cos(ang), zeros], axis=1)
    sin = jnp.concatenate([-jnp.sin(ang), jnp.sin(ang), zeros], axis=1)
    return cos, sin


def _pad_heads(w, width):
    lead = w.shape[:-1]
    w = w.reshape(lead + (HEADS, width))
    w = jnp.pad(w, [(0, 0)] * len(lead) + [(0, 0), (0, QK_PAD - width)])
    return w.reshape(lead + (HEADS * QK_PAD,))


def kernel(x, meta_tokens, norm_g, w_in, q_lora_g, kv_lora_g, w_uq, w_ukv, q_head_g, k_head_g,
           w_pool, pool_scale, w_branch_a, w_branch_b, w_out):
    assert x.shape == (BATCH, SEQ, D_MODEL) and x.dtype == F32
    assert w_in.shape == (DEPTH, D_MODEL, LAT_SRC + MAIN_W)

    w_in_t = jnp.swapaxes(w_in, 1, 2)
    w_lat = _cast_latent_weights(w_in_t)
    w_main = _cast_main_weights(w_in_t)
    w_uq_p = _pad_heads(w_uq, QK).astype(BF16)
    w_ukv_b = w_ukv.astype(BF16)
    w_pool_b = w_pool.astype(BF16)
    w_a = w_branch_a.astype(BF16)
    w_b = w_branch_b.astype(BF16)
    w_o = w_out.astype(BF16)
    head_pad = ((0, 0), (0, QK_PAD - QK))
    gq = (jnp.pad(q_head_g, head_pad) * (QK ** -0.5 * LOG2_E)).reshape(DEPTH, 1, QK_PAD)
    gk = jnp.pad(k_head_g, head_pad).reshape(DEPTH, 1, QK_PAD)
    norm_g3 = norm_g.reshape(DEPTH, 1, D_MODEL)
    gql = q_lora_g.reshape(DEPTH, 1, Q_LORA)
    gkvl = kv_lora_g.reshape(DEPTH, 1, KV_LORA)
    pscale = pool_scale.reshape(DEPTH, 1, POOL_W)
    cos, sin = _rope_tables()

    h = (x.reshape(N_REAL, D_MODEL), meta_tokens.astype(F32))
    for layer in range(DEPTH):
        hn = _rmsnorm(h, norm_g3, layer)
        proj = _inproj(hn, w_main, layer)
        lat = _latent(hn, w_lat, layer)
        q, k, v = _mla_prep(lat, gql, gkvl, w_uq_p, w_ukv_b, gq, gk, cos, sin, layer)
        ga = _attention(q, k, v, proj)
        ga = _attention(q, k, v, proj, gated_prev=ga)
        gb = _pool(proj, w_pool_b, pscale, layer)
        gb = _pool(proj, w_pool_b, pscale, layer, gated_prev=gb)
        merged = _merge(ga, gb, w_a, w_b, proj, layer)
        h = (_outproj(merged, w_o, h, layer),)
    return h[0].reshape(BATCH, SEQ, D_MODEL)
```

```python
import functools

import jax
import jax.numpy as jnp
from jax import lax
from jax.experimental import pallas as pl
from jax.experimental.pallas import tpu as pltpu

F32 = jnp.float32
BF16 = jnp.bfloat16

D_MODEL = 4096
BATCH = 4
SEQ = 4096
DEPTH = 4
CHUNK = 64
N_META = 16
HEADS = 16
NOPE = 128
ROPE = 64
QK = NOPE + ROPE
V_DIM = 128
MLA_W = HEADS * V_DIM
Q_LORA = 1024
KV_LORA = 512
POOL_W = 2048
WINDOWS = (2, 4, 8, 16)
GROUPS = len(WINDOWS)
GROUP_W = POOL_W // GROUPS
NORM_EPS = 1e-6
ROPE_THETA = 10000.0
LOG2_E = 1.4426950408889634

LANES = 128
MXU_COLS = 256
VMEM_BYTES = 64 * 1024 * 1024

N_REAL = BATCH * SEQ
N_ROWS = N_REAL + N_META
META_BLK = N_REAL // N_META
QK_PAD = MXU_COLS
LAT_W = Q_LORA + KV_LORA + LANES
MAIN_W = POOL_W + MLA_W + POOL_W + 2 * D_MODEL
LAT_SRC = Q_LORA + KV_LORA + ROPE

TM_MM = 656
TM_ROW = 400
TN = 1024
TQ = 512
TP = 256


def _vmem_limit(block_bytes, temp_bytes):
    need = 2 * sum(block_bytes) + temp_bytes
    assert need <= VMEM_BYTES - (4 << 20), need
    return int(need)


def _nbytes(shape, dtype):
    n = 1
    for s in shape:
        n *= s
    return n * jnp.dtype(dtype).itemsize


def _residual_tile(h_refs, emit):
    if len(h_refs) == 1:
        emit(h_refs[0][...])
        return
    x_ref, meta_ref = h_refs
    last = pl.num_programs(0) - 1
    n_real = N_REAL % x_ref.shape[0]
    assert n_real + N_META == x_ref.shape[0]
    pl.when(pl.program_id(0) != last)(lambda: emit(x_ref[...]))
    pl.when(pl.program_id(0) == last)(
        lambda: emit(jnp.concatenate([x_ref[0:n_real, :], meta_ref[...]], axis=0)))


def _rmsnorm_kernel(*refs):
    *h_refs, g_ref, o_ref = refs

    def emit(x):
        ms = jnp.mean(x * x, axis=-1, keepdims=True)
        o_ref[...] = (x * lax.rsqrt(ms + NORM_EPS) * g_ref[...]).astype(o_ref.dtype)

    _residual_tile(h_refs, emit)


def _rmsnorm(h_parts, norm_g, layer):
    n_tiles = N_ROWS // TM_ROW
    blocks = [_nbytes((TM_ROW, D_MODEL), F32), _nbytes((TM_ROW, D_MODEL), BF16),
              _nbytes((N_META, D_MODEL), F32)]
    h_specs = [pl.BlockSpec((TM_ROW, D_MODEL), lambda i: (i, 0))]
    if len(h_parts) == 2:
        h_specs.append(pl.BlockSpec((N_META, D_MODEL), lambda i: (0, 0)))
    return pl.pallas_call(
        _rmsnorm_kernel,
        out_shape=jax.ShapeDtypeStruct((N_ROWS, D_MODEL), BF16),
        grid=(n_tiles,),
        in_specs=h_specs + [pl.BlockSpec((None, 1, D_MODEL), lambda i: (layer, 0, 0))],
        out_specs=pl.BlockSpec((TM_ROW, D_MODEL), lambda i: (i, 0)),
        compiler_params=pltpu.CompilerParams(
            dimension_semantics=("parallel",),
            vmem_limit_bytes=_vmem_limit(blocks, 2 * _nbytes((TM_ROW, D_MODEL), F32))),
        name="rmsnorm",
    )(*h_parts, norm_g)


_RAW_TILES = POOL_W // TN
_SILU_TILES = (MLA_W + POOL_W) // TN


def _inproj_kernel(x_ref, w_ref, o_ref):
    j = pl.program_id(1)
    acc = jnp.dot(x_ref[...], w_ref[...], preferred_element_type=F32)
    sig = jax.nn.sigmoid(acc)
    out = jnp.where(j < _RAW_TILES, acc,
                    jnp.where(j < _RAW_TILES + _SILU_TILES, acc * sig, sig))
    o_ref[...] = out.astype(o_ref.dtype)


def _inproj(hn, w_main, layer):
    blocks = [_nbytes((TM_MM, D_MODEL), BF16), _nbytes((D_MODEL, TN), BF16),
              _nbytes((TM_MM, TN), BF16)]
    return pl.pallas_call(
        _inproj_kernel,
        out_shape=jax.ShapeDtypeStruct((N_ROWS, MAIN_W), BF16),
        grid=(N_ROWS // TM_MM, MAIN_W // TN),
        in_specs=[
            pl.BlockSpec((TM_MM, D_MODEL), lambda i, j: (i, 0)),
            pl.BlockSpec((None, D_MODEL, TN), lambda i, j: (layer, 0, j)),
        ],
        out_specs=pl.BlockSpec((TM_MM, TN), lambda i, j: (i, j)),
        compiler_params=pltpu.CompilerParams(
            dimension_semantics=("parallel", "parallel"),
            vmem_limit_bytes=_vmem_limit(blocks, 3 * _nbytes((TM_MM, TN), F32))),
        name="inproj",
    )(hn, w_main)


def _latent_kernel(x_ref, w_ref, o_ref):
    o_ref[...] = jnp.dot(x_ref[...], w_ref[...], preferred_element_type=F32)


def _latent(hn, w_lat, layer):
    blocks = [_nbytes((TM_ROW, D_MODEL), BF16), _nbytes((D_MODEL, LAT_W), BF16),
              _nbytes((TM_ROW, LAT_W), F32)]
    return pl.pallas_call(
        _latent_kernel,
        out_shape=jax.ShapeDtypeStruct((N_ROWS, LAT_W), F32),
        grid=(N_ROWS // TM_ROW,),
        in_specs=[
            pl.BlockSpec((TM_ROW, D_MODEL), lambda i: (i, 0)),
            pl.BlockSpec((None, D_MODEL, LAT_W), lambda i: (layer, 0, 0)),
        ],
        out_specs=pl.BlockSpec((TM_ROW, LAT_W), lambda i: (i, 0)),
        compiler_params=pltpu.CompilerParams(
            dimension_semantics=("parallel",),
            vmem_limit_bytes=_vmem_limit(blocks, _nbytes((TM_ROW, LAT_W), F32))),
        name="latent",
    )(hn, w_lat)


def _rope(x, cos, sin_signed):
    partner = pltpu.roll(x, ROPE // 2, axis=1) + pltpu.roll(x, LANES - ROPE // 2, axis=1)
    return x * cos + partner * sin_signed


def _mla_prep_kernel(lat_ref, gql_ref, gkvl_ref, wuq_ref, wukv_ref, gq_ref, gk_ref,
                     cos_ref, sin_ref, q_ref, k_ref, v_ref):
    cos = cos_ref[...]
    sin = sin_ref[...]
    cq = lat_ref[:, 0:Q_LORA]
    ckv = lat_ref[:, Q_LORA:Q_LORA + KV_LORA]
    kr = lat_ref[:, Q_LORA + KV_LORA:LAT_W]
    cqn = (cq * lax.rsqrt(jnp.mean(cq * cq, axis=-1, keepdims=True) + NORM_EPS)
           * gql_ref[...]).astype(BF16)
    ckvn = (ckv * lax.rsqrt(jnp.mean(ckv * ckv, axis=-1, keepdims=True) + NORM_EPS)
            * gkvl_ref[...]).astype(BF16)
    gq = gq_ref[...]
    gk = gk_ref[...]
    kr_ssq = jnp.sum(kr * kr, axis=-1, keepdims=True)
    kr_roped = _rope(kr * gk[:, NOPE:QK_PAD], cos, sin)
    for h in range(HEADS):
        c0 = h * QK_PAD
        qh = jnp.dot(cqn, wuq_ref[:, c0:c0 + QK_PAD], preferred_element_type=F32)
        rq = lax.rsqrt(jnp.sum(qh * qh, axis=-1, keepdims=True) * (1.0 / QK) + NORM_EPS)
        qn = qh * rq * gq
        q_ref[:, c0:c0 + NOPE] = qn[:, 0:NOPE].astype(BF16)
        q_ref[:, c0 + NOPE:c0 + QK_PAD] = _rope(qn[:, NOPE:QK_PAD], cos, sin).astype(BF16)
        kvh = jnp.dot(ckvn, wukv_ref[:, c0:c0 + QK_PAD], preferred_element_type=F32)
        kn = kvh[:, 0:NOPE]
        rk = lax.rsqrt((jnp.sum(kn * kn, axis=-1, keepdims=True) + kr_ssq) * (1.0 / QK)
                       + NORM_EPS)
        k_ref[:, c0:c0 + NOPE] = (kn * rk * gk[:, 0:NOPE]).astype(BF16)
        k_ref[:, c0 + NOPE:c0 + QK_PAD] = (kr_roped * rk).astype(BF16)
        v_ref[:, h * V_DIM:(h + 1) * V_DIM] = kvh[:, NOPE:NOPE + V_DIM].astype(BF16)


def _mla_prep(lat, gql, gkvl, wuq, wukv, gq, gk, cos, sin, layer):
    blocks = [_nbytes((TM_ROW, LAT_W), F32), _nbytes((Q_LORA, HEADS * QK_PAD), BF16),
              _nbytes((KV_LORA, HEADS * QK_PAD), BF16),
              2 * _nbytes((TM_ROW, HEADS * QK_PAD), BF16), _nbytes((TM_ROW, MLA_W), BF16),
              2 * _nbytes((TM_ROW, LANES), F32)]
    row = lambda i: (i, 0)
    par = lambda i: (layer, 0, 0)
    return pl.pallas_call(
        _mla_prep_kernel,
        out_shape=(jax.ShapeDtypeStruct((N_ROWS, HEADS * QK_PAD), BF16),
                   jax.ShapeDtypeStruct((N_ROWS, HEADS * QK_PAD), BF16),
                   jax.ShapeDtypeStruct((N_ROWS, MLA_W), BF16)),
        grid=(N_ROWS // TM_ROW,),
        in_specs=[
            pl.BlockSpec((TM_ROW, LAT_W), row),
            pl.BlockSpec((None, 1, Q_LORA), par),
            pl.BlockSpec((None, 1, KV_LORA), par),
            pl.BlockSpec((None, Q_LORA, HEADS * QK_PAD), par),
            pl.BlockSpec((None, KV_LORA, HEADS * QK_PAD), par),
            pl.BlockSpec((None, 1, QK_PAD), par),
            pl.BlockSpec((None, 1, QK_PAD), par),
            pl.BlockSpec((TM_ROW, LANES), row),
            pl.BlockSpec((TM_ROW, LANES), row),
        ],
        out_specs=(pl.BlockSpec((TM_ROW, HEADS * QK_PAD), row),
                   pl.BlockSpec((TM_ROW, HEADS * QK_PAD), row),
                   pl.BlockSpec((TM_ROW, MLA_W), row)),
        compiler_params=pltpu.CompilerParams(
            dimension_semantics=("parallel",),
            vmem_limit_bytes=_vmem_limit(
                blocks, _nbytes((TM_ROW, Q_LORA + KV_LORA), BF16)
                + 6 * _nbytes((TM_ROW, QK_PAD), F32))),
        name="mla_prep",
    )(lat, gql, gkvl, wuq, wukv, gq, gk, cos, sin)


_NT = (((1,), (1,)), ((), ()))
_MASKED = -1e30


def _row_max(x):
    return jnp.max(x, axis=-1, keepdims=True)


def _with_ones(v):
    lane = lax.broadcasted_iota(jnp.int32, (v.shape[0], LANES), 1)
    return jnp.concatenate([v, (lane == 0).astype(v.dtype)], axis=1)


def _tile_update(s, v, state, meta=None):
    row_max = _row_max(s)
    if meta is not None:
        row_max = jnp.maximum(row_max, _row_max(meta[0]))
    m_new = row_max if state is None else jnp.maximum(state[0], row_max)
    pv = jnp.dot(jnp.exp2(s - m_new).astype(BF16), v, preferred_element_type=F32)
    if meta is not None:
        p_meta = jnp.exp2(meta[0] - m_new).astype(BF16)
        pv = pv + jnp.dot(p_meta, meta[1], preferred_element_type=F32)
    if state is None:
        return m_new, pv
    return m_new, jnp.exp2(state[0] - m_new) * state[1] + pv


def _causal_bias():
    q_chunk = jnp.arange(TQ)[:, None] // CHUNK
    k_chunk = jnp.arange(TQ)[None, :] // CHUNK
    return jnp.where(k_chunk <= q_chunk, 0.0, _MASKED).astype(F32)


def _attn_kernel(q_ref, k_ref, v_ref, km_ref, vm_ref, g_ref, bias_ref, o_ref, *, real_keys):
    pad = LANES - N_META
    km = jnp.concatenate([km_ref[...], jnp.zeros((pad, QK_PAD), BF16)], axis=0)
    vm = _with_ones(jnp.concatenate([vm_ref[...], jnp.zeros((pad, V_DIM), BF16)], axis=0))

    def meta_scores(q):
        s = lax.dot_general(q, km, _NT, preferred_element_type=F32)
        return jnp.where(lax.broadcasted_iota(jnp.int32, s.shape, 1) < N_META, s, _MASKED)

    def finish(rows, acc):
        out = acc[:, 0:V_DIM] / acc[:, V_DIM:V_DIM + 1]
        o_ref[rows, :] = (out * g_ref[rows, :].astype(F32)).astype(o_ref.dtype)

    def chain(tile):
        rows = slice(tile * TQ, (tile + 1) * TQ)
        q = q_ref[rows, :]
        order = list(range(tile))
        order.insert(tile // 2, tile)

        def scores(j):
            s = lax.dot_general(q, k_ref[j * TQ:(j + 1) * TQ, :], _NT,
                                preferred_element_type=F32)
            return s + bias_ref[...] if j == tile else s

        meta = (meta_scores(q), vm)
        state = None
        s_next = scores(order[0])
        for n, j in enumerate(order):
            if n > 0:
                yield
            s = s_next
            if n + 1 < len(order):
                s_next = scores(order[n + 1])
            state = _tile_update(s, _with_ones(v_ref[j * TQ:(j + 1) * TQ, :]), state,
                                 meta=meta if n == 0 else None)
        finish(rows, state[1])

    def pair(short):
        lengths = [SEQ // TQ - short, short + 1]
        chains = [chain(SEQ // TQ - 1 - short), chain(short)]
        done = [0, 0]
        while chains[0] is not None or chains[1] is not None:
            live = [c for c in (0, 1) if chains[c] is not None]
            c = min(live, key=lambda c: (done[c] + 0.5) / lengths[c])
            try:
                next(chains[c])
                done[c] += 1
            except StopIteration:
                chains[c] = None

    if real_keys:
        step = pl.program_id(2)
        for short in range(SEQ // TQ // 2):
            pl.when(step == short)(functools.partial(pair, short))
    else:
        s = meta_scores(q_ref[...])
        p = jnp.exp2(s - _row_max(s)).astype(BF16)
        finish(slice(None), jnp.dot(p, vm, preferred_element_type=F32))


_GATE_A_BLK = POOL_W // V_DIM


def _attention(q, k, v, proj, gated_prev=None):
    if gated_prev is None:
        blocks = [2 * _nbytes((SEQ, QK_PAD), BF16), 3 * _nbytes((SEQ, V_DIM), BF16),
                  _nbytes((N_META, QK_PAD + V_DIM), BF16), _nbytes((TQ, TQ), F32)]
        head = lambda b, h, step: (b, h)
        return pl.pallas_call(
            functools.partial(_attn_kernel, real_keys=True),
            out_shape=jax.ShapeDtypeStruct((N_ROWS, MLA_W), BF16),
            grid=(BATCH, HEADS, SEQ // TQ // 2),
            in_specs=[
                pl.BlockSpec((SEQ, QK_PAD), head),
                pl.BlockSpec((SEQ, QK_PAD), head),
                pl.BlockSpec((SEQ, V_DIM), head),
                pl.BlockSpec((N_META, QK_PAD), lambda b, h, step: (META_BLK, h)),
                pl.BlockSpec((N_META, V_DIM), lambda b, h, step: (META_BLK, h)),
                pl.BlockSpec((SEQ, V_DIM), lambda b, h, step: (b, _GATE_A_BLK + h)),
                pl.BlockSpec((TQ, TQ), lambda b, h, step: (0, 0)),
            ],
            out_specs=pl.BlockSpec((SEQ, V_DIM), head),
            compiler_params=pltpu.CompilerParams(
                dimension_semantics=("parallel", "parallel", "arbitrary"),
                vmem_limit_bytes=_vmem_limit(blocks, 16 * _nbytes((TQ, TQ), F32))),
            name="attention",
        )(q, k, v, k, v, proj, _causal_bias())
    blocks = [_nbytes((N_META, QK_PAD), BF16), _nbytes((N_META, QK_PAD + V_DIM), BF16),
              2 * _nbytes((N_META, V_DIM), BF16)]
    meta = lambda h: (META_BLK, h)
    return pl.pallas_call(
        lambda q_ref, km_ref, vm_ref, g_ref, prev_ref, o_ref: _attn_kernel(
            q_ref, None, None, km_ref, vm_ref, g_ref, None, o_ref, real_keys=False),
        out_shape=jax.ShapeDtypeStruct((N_ROWS, MLA_W), BF16),
        grid=(HEADS,),
        in_specs=[
            pl.BlockSpec((N_META, QK_PAD), meta),
            pl.BlockSpec((N_META, QK_PAD), meta),
            pl.BlockSpec((N_META, V_DIM), meta),
            pl.BlockSpec((N_META, V_DIM), lambda h: (META_BLK, _GATE_A_BLK + h)),
            pl.BlockSpec(memory_space=pl.ANY),
        ],
        out_specs=pl.BlockSpec((N_META, V_DIM), meta),
        input_output_aliases={4: 0},
        compiler_params=pltpu.CompilerParams(
            dimension_semantics=("parallel",),
            vmem_limit_bytes=_vmem_limit(blocks, 8 * _nbytes((LANES, QK_PAD), F32))),
        name="attention_meta",
    )(q, k, v, proj, gated_prev)


def _pool_kernel(u_ref, halo_ref, bm_ref, bh_ref, invc_ref, wp_ref, sc_ref, g_ref, o_ref,
                 *, has_main):
    pad = LANES - N_META
    for g in range(GROUPS):
        cs = slice(g * GROUP_W, (g + 1) * GROUP_W)
        halo = jnp.concatenate([halo_ref[:, cs], jnp.zeros((pad, GROUP_W), BF16)], axis=0)
        win = jnp.dot(bh_ref[g], halo, preferred_element_type=F32)
        u = u_ref[:, cs]
        if has_main:
            win = win + jnp.dot(bm_ref[g], u, preferred_element_type=F32)
        diff = win * invc_ref[g][:, 0:1] - u.astype(F32)
        mixed = jnp.dot(diff.astype(BF16), wp_ref[g], preferred_element_type=F32)
        out = mixed * sc_ref[:, cs] * g_ref[:, cs].astype(F32)
        o_ref[:, cs] = out.astype(o_ref.dtype)


_GATE_B_BLK = (POOL_W + MLA_W) // POOL_W


def _pool_bands(rows, meta):
    t = jnp.arange(rows)[:, None]
    s = jnp.arange(rows)[None, :]
    s_h = jnp.arange(LANES)[None, :]
    main, halo, invc = [], [], []
    for w in WINDOWS:
        main.append(((t - s >= 0) & (t - s < w)).astype(BF16))
        if meta:
            halo.append(((t - s_h >= 0) & (t - s_h < w) & (s_h < N_META)).astype(BF16))
            cnt = jnp.minimum(t + 1, w).astype(F32)
        else:
            dist = t + N_META - s_h
            halo.append(((dist < w) & (s_h < N_META)).astype(BF16))
            cnt = jnp.full((rows, 1), w, F32)
        invc.append(jnp.broadcast_to(1.0 / cnt, (rows, LANES)))
    return jnp.stack(main), jnp.stack(halo), jnp.stack(invc)


def _pool(proj, w_pool, pool_scale, layer, gated_prev=None):
    meta = gated_prev is not None
    rows = N_META if meta else TP
    bm, bh, invc = _pool_bands(rows, meta)
    nt = SEQ // TP
    if meta:
        grid = (1,)
        tile = lambda t: (META_BLK, 0)
        halo = tile
        gate = lambda t: (META_BLK, _GATE_B_BLK)
        const3 = lambda t: (0, 0, 0)
        wmap = lambda t: (layer, 0, 0, 0)
        smap = lambda t: (layer, 0, 0)
    else:
        grid = (BATCH, nt)
        tile = lambda b, t: (b * nt + t, 0)
        halo = lambda b, t: (jnp.where(t == 0, META_BLK,
                                       (b * SEQ + t * TP) // N_META - 1), 0)
        gate = lambda b, t: (b * nt + t, _GATE_B_BLK)
        const3 = lambda b, t: (0, 0, 0)
        wmap = lambda b, t: (layer, 0, 0, 0)
        smap = lambda b, t: (layer, 0, 0)
    blocks = [3 * _nbytes((rows, POOL_W), BF16), _nbytes((N_META, POOL_W), BF16),
              _nbytes((GROUPS, rows, rows + LANES), BF16), _nbytes((GROUPS, rows, LANES), F32),
              _nbytes((GROUPS, GROUP_W, GROUP_W), BF16), _nbytes((1, POOL_W), F32)]
    in_specs = [
        pl.BlockSpec((rows, POOL_W), tile),
        pl.BlockSpec((N_META, POOL_W), halo),
        pl.BlockSpec((GROUPS, rows, rows), const3),
        pl.BlockSpec((GROUPS, rows, LANES), const3),
        pl.BlockSpec((GROUPS, rows, LANES), const3),
        pl.BlockSpec((None, GROUPS, GROUP_W, GROUP_W), wmap),
        pl.BlockSpec((None, 1, POOL_W), smap),
        pl.BlockSpec((rows, POOL_W), gate),
    ]
    args = [proj, proj, bm, bh, invc, w_pool, pool_scale, proj]
    kern = functools.partial(_pool_kernel, has_main=not meta)
    aliases = {}
    if meta:
        in_specs.append(pl.BlockSpec(memory_space=pl.ANY))
        args.append(gated_prev)
        aliases = {8: 0}
        inner = kern
        kern = lambda *refs: inner(*refs[:8], refs[9])
    return pl.pallas_call(
        kern,
        out_shape=jax.ShapeDtypeStruct((N_ROWS, POOL_W), BF16),
        grid=grid,
        in_specs=in_specs,
        out_specs=pl.BlockSpec((rows, POOL_W), tile),
        input_output_aliases=aliases,
        compiler_params=pltpu.CompilerParams(
            dimension_semantics=("parallel",) * len(grid),
            vmem_limit_bytes=_vmem_limit(blocks, 6 * _nbytes((max(rows, LANES), GROUP_W), F32))),
        name="pool_meta" if meta else "pool",
    )(*args)


def _merge_kernel(ga_ref, gb_ref, wa_ref, wb_ref, sa_ref, sb_ref, o_ref):
    a = jnp.dot(ga_ref[...], wa_ref[...], preferred_element_type=F32)
    b = jnp.dot(gb_ref[...], wb_ref[...], preferred_element_type=F32)
    o_ref[...] = (sa_ref[...].astype(F32) * a + sb_ref[...].astype(F32) * b).astype(o_ref.dtype)


_MERGE_A_BLK = (POOL_W + MLA_W + POOL_W) // TN
_MERGE_B_BLK = _MERGE_A_BLK + D_MODEL // TN


def _merge(ga, gb, w_a, w_b, proj, layer):
    blocks = [2 * _nbytes((TM_MM, MLA_W), BF16), 2 * _nbytes((MLA_W, TN), BF16),
              3 * _nbytes((TM_MM, TN), BF16)]
    return pl.pallas_call(
        _merge_kernel,
        out_shape=jax.ShapeDtypeStruct((N_ROWS, D_MODEL), BF16),
        grid=(N_ROWS // TM_MM, D_MODEL // TN),
        in_specs=[
            pl.BlockSpec((TM_MM, MLA_W), lambda i, j: (i, 0)),
            pl.BlockSpec((TM_MM, POOL_W), lambda i, j: (i, 0)),
            pl.BlockSpec((None, MLA_W, TN), lambda i, j: (layer, 0, j)),
            pl.BlockSpec((None, POOL_W, TN), lambda i, j: (layer, 0, j)),
            pl.BlockSpec((TM_MM, TN), lambda i, j: (i, _MERGE_A_BLK + j)),
            pl.BlockSpec((TM_MM, TN), lambda i, j: (i, _MERGE_B_BLK + j)),
        ],
        out_specs=pl.BlockSpec((TM_MM, TN), lambda i, j: (i, j)),
        compiler_params=pltpu.CompilerParams(
            dimension_semantics=("parallel", "parallel"),
            vmem_limit_bytes=_vmem_limit(blocks, 3 * _nbytes((TM_MM, TN), F32))),
        name="merge",
    )(ga, gb, w_a, w_b, proj, proj)


def _outproj_kernel(m_ref, w_ref, *refs):
    *h_refs, o_ref = refs
    delta = jnp.dot(m_ref[...], w_ref[...], preferred_element_type=F32)

    def emit(h):
        o_ref[...] = h + delta

    _residual_tile(h_refs, emit)


def _outproj(merged, w_out, h_parts, layer):
    blocks = [_nbytes((TM_MM, D_MODEL), BF16), _nbytes((D_MODEL, TN), BF16),
              2 * _nbytes((TM_MM, TN), F32), _nbytes((N_META, TN), F32)]
    out_rows = N_REAL if layer == DEPTH - 1 else N_ROWS
    h_specs = [pl.BlockSpec((TM_MM, TN), lambda i, j: (i, j))]
    if len(h_parts) == 2:
        h_specs.append(pl.BlockSpec((N_META, TN), lambda i, j: (0, j)))
    in_place = len(h_parts) == 1 and out_rows == N_ROWS
    return pl.pallas_call(
        _outproj_kernel,
        out_shape=jax.ShapeDtypeStruct((out_rows, D_MODEL), F32),
        grid=(N_ROWS // TM_MM, D_MODEL // TN),
        in_specs=[
            pl.BlockSpec((TM_MM, D_MODEL), lambda i, j: (i, 0)),
            pl.BlockSpec((None, D_MODEL, TN), lambda i, j: (layer, 0, j)),
        ] + h_specs,
        out_specs=pl.BlockSpec((TM_MM, TN), lambda i, j: (i, j)),
        input_output_aliases={2: 0} if in_place else {},
        compiler_params=pltpu.CompilerParams(
            dimension_semantics=("parallel", "parallel"),
            vmem_limit_bytes=_vmem_limit(blocks, 4 * _nbytes((TM_MM, TN), F32))),
        name="outproj",
    )(merged, w_out, *h_parts)


_CAST_K = 1024
_SHIFT = LAT_SRC % LANES


def _cast_main_kernel(a_ref, b_ref, c_ref, o_ref):
    window = jnp.concatenate([a_ref[...], b_ref[...], c_ref[...]], axis=0)
    o_ref[...] = window[_SHIFT:_SHIFT + TN, :].T.astype(o_ref.dtype)


def _cast_latent_kernel(w_ref, o_ref):
    col = lax.broadcasted_iota(jnp.int32, w_ref.shape, 0)
    o_ref[...] = jnp.where(col < LAT_SRC, w_ref[...], 0.0).T.astype(o_ref.dtype)


def _cast_latent_weights(w_in_t):
    blocks = [_nbytes((LAT_W, _CAST_K), F32), _nbytes((_CAST_K, LAT_W), BF16)]
    return pl.pallas_call(
        _cast_latent_kernel,
        out_shape=jax.ShapeDtypeStruct((DEPTH, D_MODEL, LAT_W), BF16),
        grid=(DEPTH, D_MODEL // _CAST_K),
        in_specs=[pl.BlockSpec((None, LAT_W, _CAST_K), lambda l, r: (l, 0, r))],
        out_specs=pl.BlockSpec((None, _CAST_K, LAT_W), lambda l, r: (l, r, 0)),
        compiler_params=pltpu.CompilerParams(
            dimension_semantics=("parallel", "parallel"),
            vmem_limit_bytes=_vmem_limit(blocks, 3 * _nbytes((_CAST_K, LAT_W), F32))),
        name="cast_latent",
    )(w_in_t)


def _cast_main_weights(w_in_t):
    base = LAT_SRC - _SHIFT
    half = TN // 2
    assert base % half == 0 and (base + TN) % LANES == 0
    blocks = [2 * _nbytes((half, _CAST_K), F32), _nbytes((LANES, _CAST_K), F32),
              _nbytes((_CAST_K, TN), BF16)]
    return pl.pallas_call(
        _cast_main_kernel,
        out_shape=jax.ShapeDtypeStruct((DEPTH, D_MODEL, MAIN_W), BF16),
        grid=(DEPTH, D_MODEL // _CAST_K, MAIN_W // TN),
        in_specs=[
            pl.BlockSpec((None, half, _CAST_K), lambda l, r, j: (l, base // half + 2 * j, r)),
            pl.BlockSpec((None, half, _CAST_K),
                         lambda l, r, j: (l, base // half + 1 + 2 * j, r)),
            pl.BlockSpec((None, LANES, _CAST_K),
                         lambda l, r, j: (l, (base + TN) // LANES + (TN // LANES) * j, r)),
        ],
        out_specs=pl.BlockSpec((None, _CAST_K, TN), lambda l, r, j: (l, r, j)),
        compiler_params=pltpu.CompilerParams(
            dimension_semantics=("parallel", "parallel", "parallel"),
            vmem_limit_bytes=_vmem_limit(blocks, 3 * _nbytes((TN + LANES, _CAST_K), F32))),
        name="cast_main",
    )(w_in_t, w_in_t, w_in_t)


def _rope_tables():
    pos = jnp.concatenate([
        jnp.tile(jnp.arange(SEQ, dtype=F32) + N_META, BATCH),
        jnp.arange(N_META, dtype=F32)])
    inv = 1.0 / (ROPE_THETA ** (jnp.arange(0, ROPE, 2, dtype=F32) / ROPE))
    ang = pos[:, None] * inv[None, :]
    zeros = jnp.zeros((N_ROWS, LANES - ROPE), F32)
    cos = jnp.concatenate([jnp.cos(ang), jnp.cos(ang), zeros], axis=1)
    sin = jnp.concatenate([-jnp.sin(ang), jnp.sin(ang), zeros], axis=1)
    return cos, sin


def _pad_heads(w, width):
    lead = w.shape[:-1]
    w = w.reshape(lead + (HEADS, width))
    w = jnp.pad(w, [(0, 0)] * len(lead) + [(0, 0), (0, QK_PAD - width)])
    return w.reshape(lead + (HEADS * QK_PAD,))


def kernel(x, meta_tokens, norm_g, w_in, q_lora_g, kv_lora_g, w_uq, w_ukv, q_head_g, k_head_g,
           w_pool, pool_scale, w_branch_a, w_branch_b, w_out):
    assert x.shape == (BATCH, SEQ, D_MODEL) and x.dtype == F32
    assert w_in.shape == (DEPTH, D_MODEL, LAT_SRC + MAIN_W)

    w_in_t = jnp.swapaxes(w_in, 1, 2)
    w_lat = _cast_latent_weights(w_in_t)
    w_main = _cast_main_weights(w_in_t)
    w_uq_p = _pad_heads(w_uq, QK).astype(BF16)
    w_ukv_b = w_ukv.astype(BF16)
    w_pool_b = w_pool.astype(BF16)
    w_a = w_branch_a.astype(BF16)
    w_b = w_branch_b.astype(BF16)
    w_o = w_out.astype(BF16)
    head_pad = ((0, 0), (0, QK_PAD - QK))
    gq = (jnp.pad(q_head_g, head_pad) * (QK ** -0.5 * LOG2_E)).reshape(DEPTH, 1, QK_PAD)
    gk = jnp.pad(k_head_g, head_pad).reshape(DEPTH, 1, QK_PAD)
    norm_g3 = norm_g.reshape(DEPTH, 1, D_MODEL)
    gql = q_lora_g.reshape(DEPTH, 1, Q_LORA)
    gkvl = kv_lora_g.reshape(DEPTH, 1, KV_LORA)
    pscale = pool_scale.reshape(DEPTH, 1, POOL_W)
    cos, sin = _rope_tables()

    h = (x.reshape(N_REAL, D_MODEL), meta_tokens.astype(F32))
    for layer in range(DEPTH):
        hn = _rmsnorm(h, norm_g3, layer)
        proj = _inproj(hn, w_main, layer)
        lat = _latent(hn, w_lat, layer)
        q, k, v = _mla_prep(lat, gql, gkvl, w_uq_p, w_ukv_b, gq, gk, cos, sin, layer)
        ga = _attention(q, k, v, proj)
        ga = _attention(q, k, v, proj, gated_prev=ga)
        gb = _pool(proj, w_pool_b, pscale, layer)
        gb = _pool(proj, w_pool_b, pscale, layer, gated_prev=gb)
        merged = _merge(ga, gb, w_a, w_b, proj, layer)
        h = (_outproj(merged, w_o, h, layer),)
    return h[0].reshape(BATCH, SEQ, D_MODEL)
```

```python
import functools

import jax
import jax.numpy as jnp
from jax import lax
from jax.experimental import pallas as pl
from jax.experimental.pallas import tpu as pltpu

F32 = jnp.float32
BF16 = jnp.bfloat16

D_MODEL = 4096
BATCH = 4
SEQ = 4096
DEPTH = 4
CHUNK = 64
N_META = 16
HEADS = 16
NOPE = 128
ROPE = 64
QK = NOPE + ROPE
V_DIM = 128
MLA_W = HEADS * V_DIM
Q_LORA = 1024
KV_LORA = 512
POOL_W = 2048
WINDOWS = (2, 4, 8, 16)
GROUPS = len(WINDOWS)
GROUP_W = POOL_W // GROUPS
NORM_EPS = 1e-6
ROPE_THETA = 10000.0
LOG2_E = 1.4426950408889634

LANES = 128
MXU_COLS = 256
VMEM_BYTES = 64 * 1024 * 1024

N_REAL = BATCH * SEQ
N_ROWS = N_REAL + N_META
META_BLK = N_REAL // N_META
QK_PAD = MXU_COLS
LAT_W = Q_LORA + KV_LORA + LANES
MAIN_W = POOL_W + MLA_W + POOL_W + 2 * D_MODEL
LAT_SRC = Q_LORA + KV_LORA + ROPE

TM_MM = 656
TM_ROW = 400
TN = 1024
TQ = 512
TP = 256


def _vmem_limit(block_bytes, temp_bytes):
    need = 2 * sum(block_bytes) + temp_bytes
    assert need <= VMEM_BYTES - (4 << 20), need
    return int(need)


def _nbytes(shape, dtype):
    n = 1
    for s in shape:
        n *= s
    return n * jnp.dtype(dtype).itemsize


def _row_tiles(arrays, emit):
    split = [refs for refs in arrays if len(refs) == 2]
    if not split:
        emit(*[refs[0][...] for refs in arrays])
        return
    tile_rows = split[0][0].shape[0]
    n_real = N_REAL % tile_rows
    assert n_real + N_META == tile_rows

    def last_tile(refs):
        if len(refs) == 1:
            return refs[0][...]
        return jnp.concatenate([refs[0][0:n_real, :], refs[1][...]], axis=0)

    last = pl.num_programs(0) - 1
    pl.when(pl.program_id(0) != last)(lambda: emit(*[refs[0][...] for refs in arrays]))
    pl.when(pl.program_id(0) == last)(lambda: emit(*[last_tile(refs) for refs in arrays]))


def _rmsnorm_kernel(*refs):
    *h_refs, g_ref, o_ref = refs

    def emit(x):
        ms = jnp.mean(x * x, axis=-1, keepdims=True)
        o_ref[...] = (x * lax.rsqrt(ms + NORM_EPS) * g_ref[...]).astype(o_ref.dtype)

    _row_tiles([tuple(h_refs)], emit)


def _rmsnorm(h_parts, norm_g, layer):
    n_tiles = N_ROWS // TM_ROW
    blocks = [_nbytes((TM_ROW, D_MODEL), F32), _nbytes((TM_ROW, D_MODEL), BF16),
              _nbytes((N_META, D_MODEL), F32)]
    h_specs = [pl.BlockSpec((TM_ROW, D_MODEL), lambda i: (i, 0))]
    if len(h_parts) == 2:
        h_specs.append(pl.BlockSpec((N_META, D_MODEL), lambda i: (0, 0)))
    return pl.pallas_call(
        _rmsnorm_kernel,
        out_shape=jax.ShapeDtypeStruct((N_ROWS, D_MODEL), BF16),
        grid=(n_tiles,),
        in_specs=h_specs + [pl.BlockSpec((None, 1, D_MODEL), lambda i: (layer, 0, 0))],
        out_specs=pl.BlockSpec((TM_ROW, D_MODEL), lambda i: (i, 0)),
        compiler_params=pltpu.CompilerParams(
            dimension_semantics=("parallel",),
            vmem_limit_bytes=_vmem_limit(blocks, 2 * _nbytes((TM_ROW, D_MODEL), F32))),
        name="rmsnorm",
    )(*h_parts, norm_g)


_RAW_TILES = POOL_W // TN
_SILU_TILES = (MLA_W + POOL_W) // TN


def _inproj_kernel(x_ref, w_ref, o_ref):
    j = pl.program_id(1)
    acc = jnp.dot(x_ref[...], w_ref[...], preferred_element_type=F32)
    sig = jax.nn.sigmoid(acc)
    out = jnp.where(j < _RAW_TILES, acc,
                    jnp.where(j < _RAW_TILES + _SILU_TILES, acc * sig, sig))
    o_ref[...] = out.astype(o_ref.dtype)


def _inproj(hn, w_main, layer):
    blocks = [_nbytes((TM_MM, D_MODEL), BF16), _nbytes((D_MODEL, TN), BF16),
              _nbytes((TM_MM, TN), BF16)]
    return pl.pallas_call(
        _inproj_kernel,
        out_shape=jax.ShapeDtypeStruct((N_ROWS, MAIN_W), BF16),
        grid=(N_ROWS // TM_MM, MAIN_W // TN),
        in_specs=[
            pl.BlockSpec((TM_MM, D_MODEL), lambda i, j: (i, 0)),
            pl.BlockSpec((None, D_MODEL, TN), lambda i, j: (layer, 0, j)),
        ],
        out_specs=pl.BlockSpec((TM_MM, TN), lambda i, j: (i, j)),
        compiler_params=pltpu.CompilerParams(
            dimension_semantics=("parallel", "parallel"),
            vmem_limit_bytes=_vmem_limit(blocks, 3 * _nbytes((TM_MM, TN), F32))),
        name="inproj",
    )(hn, w_main)


def _latent_kernel(x_ref, w_ref, o_ref):
    o_ref[...] = jnp.dot(x_ref[...], w_ref[...], preferred_element_type=F32)


def _latent(hn, w_lat, layer):
    blocks = [_nbytes((TM_ROW, D_MODEL), BF16), _nbytes((D_MODEL, LAT_W), BF16),
              _nbytes((TM_ROW, LAT_W), F32)]
    return pl.pallas_call(
        _latent_kernel,
        out_shape=jax.ShapeDtypeStruct((N_ROWS, LAT_W), F32),
        grid=(N_ROWS // TM_ROW,),
        in_specs=[
            pl.BlockSpec((TM_ROW, D_MODEL), lambda i: (i, 0)),
            pl.BlockSpec((None, D_MODEL, LAT_W), lambda i: (layer, 0, 0)),
        ],
        out_specs=pl.BlockSpec((TM_ROW, LAT_W), lambda i: (i, 0)),
        compiler_params=pltpu.CompilerParams(
            dimension_semantics=("parallel",),
            vmem_limit_bytes=_vmem_limit(blocks, _nbytes((TM_ROW, LAT_W), F32))),
        name="latent",
    )(hn, w_lat)


def _rope(x, cos, sin_signed):
    partner = pltpu.roll(x, ROPE // 2, axis=1) + pltpu.roll(x, LANES - ROPE // 2, axis=1)
    return x * cos + partner * sin_signed


def _mla_prep_kernel(lat_ref, gql_ref, gkvl_ref, wuq_ref, wukv_ref, gq_ref, gk_ref,
                     cos_ref, sin_ref, q_ref, k_ref, v_ref):
    cos = cos_ref[...]
    sin = sin_ref[...]
    cq = lat_ref[:, 0:Q_LORA]
    ckv = lat_ref[:, Q_LORA:Q_LORA + KV_LORA]
    kr = lat_ref[:, Q_LORA + KV_LORA:LAT_W]
    cqn = (cq * lax.rsqrt(jnp.mean(cq * cq, axis=-1, keepdims=True) + NORM_EPS)
           * gql_ref[...]).astype(BF16)
    ckvn = (ckv * lax.rsqrt(jnp.mean(ckv * ckv, axis=-1, keepdims=True) + NORM_EPS)
            * gkvl_ref[...]).astype(BF16)
    gq = gq_ref[...]
    gk = gk_ref[...]
    kr_ssq = jnp.sum(kr * kr, axis=-1, keepdims=True)
    kr_roped = _rope(kr * gk[:, NOPE:QK_PAD], cos, sin)
    for h in range(HEADS):
        c0 = h * QK_PAD
        qh = jnp.dot(cqn, wuq_ref[:, c0:c0 + QK_PAD], preferred_element_type=F32)
        rq = lax.rsqrt(jnp.sum(qh * qh, axis=-1, keepdims=True) * (1.0 / QK) + NORM_EPS)
        qn = qh * rq * gq
        q_ref[:, c0:c0 + NOPE] = qn[:, 0:NOPE].astype(BF16)
        q_ref[:, c0 + NOPE:c0 + QK_PAD] = _rope(qn[:, NOPE:QK_PAD], cos, sin).astype(BF16)
        kvh = jnp.dot(ckvn, wukv_ref[:, c0:c0 + QK_PAD], preferred_element_type=F32)
        kn = kvh[:, 0:NOPE]
        rk = lax.rsqrt((jnp.sum(kn * kn, axis=-1, keepdims=True) + kr_ssq) * (1.0 / QK)
                       + NORM_EPS)
        k_ref[:, c0:c0 + NOPE] = (kn * rk * gk[:, 0:NOPE]).astype(BF16)
        k_ref[:, c0 + NOPE:c0 + QK_PAD] = (kr_roped * rk).astype(BF16)
        v_ref[:, h * V_DIM:(h + 1) * V_DIM] = kvh[:, NOPE:NOPE + V_DIM].astype(BF16)


def _mla_prep(lat, gql, gkvl, wuq, wukv, gq, gk, cos, sin, layer):
    blocks = [_nbytes((TM_ROW, LAT_W), F32), _nbytes((Q_LORA, HEADS * QK_PAD), BF16),
              _nbytes((KV_LORA, HEADS * QK_PAD), BF16),
              2 * _nbytes((TM_ROW, HEADS * QK_PAD), BF16), _nbytes((TM_ROW, MLA_W), BF16),
              2 * _nbytes((TM_ROW, LANES), F32)]
    row = lambda i: (i, 0)
    par = lambda i: (layer, 0, 0)
    return pl.pallas_call(
        _mla_prep_kernel,
        out_shape=(jax.ShapeDtypeStruct((N_ROWS, HEADS * QK_PAD), BF16),
                   jax.ShapeDtypeStruct((N_ROWS, HEADS * QK_PAD), BF16),
                   jax.ShapeDtypeStruct((N_ROWS, MLA_W), BF16)),
        grid=(N_ROWS // TM_ROW,),
        in_specs=[
            pl.BlockSpec((TM_ROW, LAT_W), row),
            pl.BlockSpec((None, 1, Q_LORA), par),
            pl.BlockSpec((None, 1, KV_LORA), par),
            pl.BlockSpec((None, Q_LORA, HEADS * QK_PAD), par),
            pl.BlockSpec((None, KV_LORA, HEADS * QK_PAD), par),
            pl.BlockSpec((None, 1, QK_PAD), par),
            pl.BlockSpec((None, 1, QK_PAD), par),
            pl.BlockSpec((TM_ROW, LANES), row),
            pl.BlockSpec((TM_ROW, LANES), row),
        ],
        out_specs=(pl.BlockSpec((TM_ROW, HEADS * QK_PAD), row),
                   pl.BlockSpec((TM_ROW, HEADS * QK_PAD), row),
                   pl.BlockSpec((TM_ROW, MLA_W), row)),
        compiler_params=pltpu.CompilerParams(
            dimension_semantics=("parallel",),
            vmem_limit_bytes=_vmem_limit(
                blocks, _nbytes((TM_ROW, Q_LORA + KV_LORA), BF16)
                + 6 * _nbytes((TM_ROW, QK_PAD), F32))),
        name="mla_prep",
    )(lat, gql, gkvl, wuq, wukv, gq, gk, cos, sin)


_NT = (((1,), (1,)), ((), ()))
_MASKED = -1e30


def _row_max(x):
    return jnp.max(x, axis=-1, keepdims=True)


def _with_ones(v):
    lane = lax.broadcasted_iota(jnp.int32, (v.shape[0], LANES), 1)
    return jnp.concatenate([v, (lane == 0).astype(v.dtype)], axis=1)


def _tile_update(s, v, state, meta=None):
    row_max = _row_max(s)
    if meta is not None:
        row_max = jnp.maximum(row_max, _row_max(meta[0]))
    m_new = row_max if state is None else jnp.maximum(state[0], row_max)
    pv = jnp.dot(jnp.exp2(s - m_new).astype(BF16), v, preferred_element_type=F32)
    if meta is not None:
        p_meta = jnp.exp2(meta[0] - m_new).astype(BF16)
        pv = pv + jnp.dot(p_meta, meta[1], preferred_element_type=F32)
    if state is None:
        return m_new, pv
    return m_new, jnp.exp2(state[0] - m_new) * state[1] + pv


def _causal_bias():
    q_chunk = jnp.arange(TQ)[:, None] // CHUNK
    k_chunk = jnp.arange(TQ)[None, :] // CHUNK
    return jnp.where(k_chunk <= q_chunk, 0.0, _MASKED).astype(F32)


def _attn_kernel(q_ref, k_ref, v_ref, km_ref, vm_ref, g_ref, bias_ref, o_ref, *, real_keys):
    pad = LANES - N_META
    km = jnp.concatenate([km_ref[...], jnp.zeros((pad, QK_PAD), BF16)], axis=0)
    vm = _with_ones(jnp.concatenate([vm_ref[...], jnp.zeros((pad, V_DIM), BF16)], axis=0))

    def meta_scores(q):
        s = lax.dot_general(q, km, _NT, preferred_element_type=F32)
        return jnp.where(lax.broadcasted_iota(jnp.int32, s.shape, 1) < N_META, s, _MASKED)

    def finish(rows, acc):
        out = acc[:, 0:V_DIM] / acc[:, V_DIM:V_DIM + 1]
        o_ref[rows, :] = (out * g_ref[rows, :].astype(F32)).astype(o_ref.dtype)

    def chain(tile):
        rows = slice(tile * TQ, (tile + 1) * TQ)
        q = q_ref[rows, :]
        order = list(range(tile))
        order.insert(tile // 2, tile)

        def scores(j):
            s = lax.dot_general(q, k_ref[j * TQ:(j + 1) * TQ, :], _NT,
                                preferred_element_type=F32)
            return s + bias_ref[...] if j == tile else s

        meta = (meta_scores(q), vm)
        state = None
        s_next = scores(order[0])
        for n, j in enumerate(order):
            if n > 0:
                yield
            s = s_next
            if n + 1 < len(order):
                s_next = scores(order[n + 1])
            state = _tile_update(s, _with_ones(v_ref[j * TQ:(j + 1) * TQ, :]), state,
                                 meta=meta if n == 0 else None)
        finish(rows, state[1])

    def pair(short):
        lengths = [SEQ // TQ - short, short + 1]
        chains = [chain(SEQ // TQ - 1 - short), chain(short)]
        done = [0, 0]
        while chains[0] is not None or chains[1] is not None:
            live = [c for c in (0, 1) if chains[c] is not None]
            c = min(live, key=lambda c: (done[c] + 0.5) / lengths[c])
            try:
                next(chains[c])
                done[c] += 1
            except StopIteration:
                chains[c] = None

    if real_keys:
        step = pl.program_id(2)
        for short in range(SEQ // TQ // 2):
            pl.when(step == short)(functools.partial(pair, short))
    else:
        s = meta_scores(q_ref[...])
        p = jnp.exp2(s - _row_max(s)).astype(BF16)
        finish(slice(None), jnp.dot(p, vm, preferred_element_type=F32))


_GATE_A_BLK = POOL_W // V_DIM


def _attention(q, k, v, proj, meta_queries):
    if not meta_queries:
        blocks = [2 * _nbytes((SEQ, QK_PAD), BF16), 3 * _nbytes((SEQ, V_DIM), BF16),
                  _nbytes((N_META, QK_PAD + V_DIM), BF16), _nbytes((TQ, TQ), F32)]
        head = lambda b, h, step: (b, h)
        return pl.pallas_call(
            functools.partial(_attn_kernel, real_keys=True),
            out_shape=jax.ShapeDtypeStruct((N_REAL, MLA_W), BF16),
            grid=(BATCH, HEADS, SEQ // TQ // 2),
            in_specs=[
                pl.BlockSpec((SEQ, QK_PAD), head),
                pl.BlockSpec((SEQ, QK_PAD), head),
                pl.BlockSpec((SEQ, V_DIM), head),
                pl.BlockSpec((N_META, QK_PAD), lambda b, h, step: (META_BLK, h)),
                pl.BlockSpec((N_META, V_DIM), lambda b, h, step: (META_BLK, h)),
                pl.BlockSpec((SEQ, V_DIM), lambda b, h, step: (b, _GATE_A_BLK + h)),
                pl.BlockSpec((TQ, TQ), lambda b, h, step: (0, 0)),
            ],
            out_specs=pl.BlockSpec((SEQ, V_DIM), head),
            compiler_params=pltpu.CompilerParams(
                dimension_semantics=("parallel", "parallel", "arbitrary"),
                vmem_limit_bytes=_vmem_limit(blocks, 16 * _nbytes((TQ, TQ), F32))),
            name="attention",
        )(q, k, v, k, v, proj, _causal_bias())
    blocks = [_nbytes((N_META, QK_PAD), BF16), _nbytes((N_META, QK_PAD + V_DIM), BF16),
              2 * _nbytes((N_META, V_DIM), BF16)]
    meta = lambda h: (META_BLK, h)
    return pl.pallas_call(
        lambda q_ref, km_ref, vm_ref, g_ref, o_ref: _attn_kernel(
            q_ref, None, None, km_ref, vm_ref, g_ref, None, o_ref, real_keys=False),
        out_shape=jax.ShapeDtypeStruct((N_META, MLA_W), BF16),
        grid=(HEADS,),
        in_specs=[
            pl.BlockSpec((N_META, QK_PAD), meta),
            pl.BlockSpec((N_META, QK_PAD), meta),
            pl.BlockSpec((N_META, V_DIM), meta),
            pl.BlockSpec((N_META, V_DIM), lambda h: (META_BLK, _GATE_A_BLK + h)),
        ],
        out_specs=pl.BlockSpec((N_META, V_DIM), lambda h: (0, h)),
        compiler_params=pltpu.CompilerParams(
            dimension_semantics=("parallel",),
            vmem_limit_bytes=_vmem_limit(blocks, 8 * _nbytes((LANES, QK_PAD), F32))),
        name="attention_meta",
    )(q, k, v, proj)


def _pool_kernel(u_ref, halo_ref, bm_ref, bh_ref, invc_ref, wp_ref, sc_ref, g_ref, o_ref,
                 *, has_main):
    pad = LANES - N_META
    for g in range(GROUPS):
        cs = slice(g * GROUP_W, (g + 1) * GROUP_W)
        halo = jnp.concatenate([halo_ref[:, cs], jnp.zeros((pad, GROUP_W), BF16)], axis=0)
        win = jnp.dot(bh_ref[g], halo, preferred_element_type=F32)
        u = u_ref[:, cs]
        if has_main:
            main = jnp.dot(bm_ref[g], u, preferred_element_type=F32)
            win = jnp.concatenate([main[0:N_META] + win, main[N_META:]], axis=0)
        diff = win * invc_ref[g][:, 0:1] - u.astype(F32)
        mixed = jnp.dot(diff.astype(BF16), wp_ref[g], preferred_element_type=F32)
        out = mixed * sc_ref[:, cs] * g_ref[:, cs].astype(F32)
        o_ref[:, cs] = out.astype(o_ref.dtype)


_GATE_B_BLK = (POOL_W + MLA_W) // POOL_W


def _pool_bands(rows, meta):
    t = jnp.arange(rows)[:, None]
    s = jnp.arange(rows)[None, :]
    t_h = jnp.arange(N_META)[:, None]
    s_h = jnp.arange(LANES)[None, :]
    main, halo, invc = [], [], []
    for w in WINDOWS:
        main.append(((t - s >= 0) & (t - s < w)).astype(BF16))
        if meta:
            halo.append(((t_h - s_h >= 0) & (t_h - s_h < w) & (s_h < N_META)).astype(BF16))
            cnt = jnp.minimum(t + 1, w).astype(F32)
        else:
            dist = t_h + N_META - s_h
            halo.append(((dist < w) & (s_h < N_META)).astype(BF16))
            cnt = jnp.full((rows, 1), w, F32)
        invc.append(jnp.broadcast_to(1.0 / cnt, (rows, LANES)))
    return jnp.stack(main), jnp.stack(halo), jnp.stack(invc)


def _pool(proj, w_pool, pool_scale, layer, meta):
    rows = N_META if meta else TP
    bm, bh, invc = _pool_bands(rows, meta)
    nt = SEQ // TP
    if meta:
        grid = (1,)
        tile = lambda t: (META_BLK, 0)
        halo = tile
        gate = lambda t: (META_BLK, _GATE_B_BLK)
        out = lambda t: (0, 0)
        const3 = lambda t: (0, 0, 0)
        wmap = lambda t: (layer, 0, 0, 0)
        smap = lambda t: (layer, 0, 0)
    else:
        grid = (BATCH, nt)
        tile = lambda b, t: (b * nt + t, 0)
        halo = lambda b, t: (jnp.where(t == 0, META_BLK,
                                       (b * SEQ + t * TP) // N_META - 1), 0)
        gate = lambda b, t: (b * nt + t, _GATE_B_BLK)
        out = tile
        const3 = lambda b, t: (0, 0, 0)
        wmap = lambda b, t: (layer, 0, 0, 0)
        smap = lambda b, t: (layer, 0, 0)
    blocks = [3 * _nbytes((rows, POOL_W), BF16), _nbytes((N_META, POOL_W), BF16),
              _nbytes((GROUPS, rows, rows), BF16), _nbytes((GROUPS, N_META, LANES), BF16),
              _nbytes((GROUPS, rows, LANES), F32),
              _nbytes((GROUPS, GROUP_W, GROUP_W), BF16), _nbytes((1, POOL_W), F32)]
    return pl.pallas_call(
        functools.partial(_pool_kernel, has_main=not meta),
        out_shape=jax.ShapeDtypeStruct((N_META if meta else N_REAL, POOL_W), BF16),
        grid=grid,
        in_specs=[
            pl.BlockSpec((rows, POOL_W), tile),
            pl.BlockSpec((N_META, POOL_W), halo),
            pl.BlockSpec((GROUPS, rows, rows), const3),
            pl.BlockSpec((GROUPS, N_META, LANES), const3),
            pl.BlockSpec((GROUPS, rows, LANES), const3),
            pl.BlockSpec((None, GROUPS, GROUP_W, GROUP_W), wmap),
            pl.BlockSpec((None, 1, POOL_W), smap),
            pl.BlockSpec((rows, POOL_W), gate),
        ],
        out_specs=pl.BlockSpec((rows, POOL_W), out),
        compiler_params=pltpu.CompilerParams(
            dimension_semantics=("parallel",) * len(grid),
            vmem_limit_bytes=_vmem_limit(blocks, 6 * _nbytes((max(rows, LANES), GROUP_W), F32))),
        name="pool_meta" if meta else "pool",
    )(proj, proj, bm, bh, invc, w_pool, pool_scale, proj)


def _merge_kernel(ga_ref, ga_meta_ref, gb_ref, gb_meta_ref, wa_ref, wb_ref, sa_ref, sb_ref,
                  o_ref):
    def emit(ga, gb):
        a = jnp.dot(ga, wa_ref[...], preferred_element_type=F32)
        b = jnp.dot(gb, wb_ref[...], preferred_element_type=F32)
        o_ref[...] = (sa_ref[...].astype(F32) * a
                      + sb_ref[...].astype(F32) * b).astype(o_ref.dtype)

    _row_tiles([(ga_ref, ga_meta_ref), (gb_ref, gb_meta_ref)], emit)


_MERGE_A_BLK = (POOL_W + MLA_W + POOL_W) // TN
_MERGE_B_BLK = _MERGE_A_BLK + D_MODEL // TN


def _merge(ga, ga_meta, gb, gb_meta, w_a, w_b, proj, layer):
    blocks = [2 * _nbytes((TM_MM, MLA_W), BF16), 2 * _nbytes((MLA_W, TN), BF16),
              3 * _nbytes((TM_MM, TN), BF16), 2 * _nbytes((N_META, MLA_W), BF16)]
    return pl.pallas_call(
        _merge_kernel,
        out_shape=jax.ShapeDtypeStruct((N_ROWS, D_MODEL), BF16),
        grid=(N_ROWS // TM_MM, D_MODEL // TN),
        in_specs=[
            pl.BlockSpec((TM_MM, MLA_W), lambda i, j: (i, 0)),
            pl.BlockSpec((N_META, MLA_W), lambda i, j: (0, 0)),
            pl.BlockSpec((TM_MM, POOL_W), lambda i, j: (i, 0)),
            pl.BlockSpec((N_META, POOL_W), lambda i, j: (0, 0)),
            pl.BlockSpec((None, MLA_W, TN), lambda i, j: (layer, 0, j)),
            pl.BlockSpec((None, POOL_W, TN), lambda i, j: (layer, 0, j)),
            pl.BlockSpec((TM_MM, TN), lambda i, j: (i, _MERGE_A_BLK + j)),
            pl.BlockSpec((TM_MM, TN), lambda i, j: (i, _MERGE_B_BLK + j)),
        ],
        out_specs=pl.BlockSpec((TM_MM, TN), lambda i, j: (i, j)),
        compiler_params=pltpu.CompilerParams(
            dimension_semantics=("parallel", "parallel"),
            vmem_limit_bytes=_vmem_limit(blocks, 3 * _nbytes((TM_MM, TN), F32)
                                         + 2 * _nbytes((TM_MM, MLA_W), BF16))),
        name="merge",
    )(ga, ga_meta, gb, gb_meta, w_a, w_b, proj, proj)


def _outproj_kernel(m_ref, w_ref, *refs):
    *h_refs, o_ref = refs
    delta = jnp.dot(m_ref[...], w_ref[...], preferred_element_type=F32)

    def emit(h):
        o_ref[...] = h + delta

    _row_tiles([tuple(h_refs)], emit)


def _outproj(merged, w_out, h_parts, layer):
    blocks = [_nbytes((TM_MM, D_MODEL), BF16), _nbytes((D_MODEL, TN), BF16),
              2 * _nbytes((TM_MM, TN), F32), _nbytes((N_META, TN), F32)]
    out_rows = N_REAL if layer == DEPTH - 1 else N_ROWS
    h_specs = [pl.BlockSpec((TM_MM, TN), lambda i, j: (i, j))]
    if len(h_parts) == 2:
        h_specs.append(pl.BlockSpec((N_META, TN), lambda i, j: (0, j)))
    in_place = len(h_parts) == 1 and out_rows == N_ROWS
    return pl.pallas_call(
        _outproj_kernel,
        out_shape=jax.ShapeDtypeStruct((out_rows, D_MODEL), F32),
        grid=(N_ROWS // TM_MM, D_MODEL // TN),
        in_specs=[
            pl.BlockSpec((TM_MM, D_MODEL), lambda i, j: (i, 0)),
            pl.BlockSpec((None, D_MODEL, TN), lambda i, j: (layer, 0, j)),
        ] + h_specs,
        out_specs=pl.BlockSpec((TM_MM, TN), lambda i, j: (i, j)),
        input_output_aliases={2: 0} if in_place else {},
        compiler_params=pltpu.CompilerParams(
            dimension_semantics=("parallel", "parallel"),
            vmem_limit_bytes=_vmem_limit(blocks, 4 * _nbytes((TM_MM, TN), F32))),
        name="outproj",
    )(merged, w_out, *h_parts)


_CAST_K = 1024
_SHIFT = LAT_SRC % LANES


def _cast_main_kernel(a_ref, b_ref, c_ref, o_ref):
    window = jnp.concatenate([a_ref[...], b_ref[...], c_ref[...]], axis=0)
    o_ref[...] = window[_SHIFT:_SHIFT + TN, :].T.astype(o_ref.dtype)


def _cast_latent_kernel(w_ref, o_ref):
    col = lax.broadcasted_iota(jnp.int32, w_ref.shape, 0)
    o_ref[...] = jnp.where(col < LAT_SRC, w_ref[...], 0.0).T.astype(o_ref.dtype)


def _cast_latent_weights(w_in_t):
    blocks = [_nbytes((LAT_W, _CAST_K), F32), _nbytes((_CAST_K, LAT_W), BF16)]
    return pl.pallas_call(
        _cast_latent_kernel,
        out_shape=jax.ShapeDtypeStruct((DEPTH, D_MODEL, LAT_W), BF16),
        grid=(DEPTH, D_MODEL // _CAST_K),
        in_specs=[pl.BlockSpec((None, LAT_W, _CAST_K), lambda l, r: (l, 0, r))],
        out_specs=pl.BlockSpec((None, _CAST_K, LAT_W), lambda l, r: (l, r, 0)),
        compiler_params=pltpu.CompilerParams(
            dimension_semantics=("parallel", "parallel"),
            vmem_limit_bytes=_vmem_limit(blocks, 3 * _nbytes((_CAST_K, LAT_W), F32))),
        name="cast_latent",
    )(w_in_t)


def _cast_main_weights(w_in_t):
    base = LAT_SRC - _SHIFT
    half = TN // 2
    assert base % half == 0 and (base + TN) % LANES == 0
    blocks = [2 * _nbytes((half, _CAST_K), F32), _nbytes((LANES, _CAST_K), F32),
              _nbytes((_CAST_K, TN), BF16)]
    return pl.pallas_call(
        _cast_main_kernel,
        out_shape=jax.ShapeDtypeStruct((DEPTH, D_MODEL, MAIN_W), BF16),
        grid=(DEPTH, D_MODEL // _CAST_K, MAIN_W // TN),
        in_specs=[
            pl.BlockSpec((None, half, _CAST_K), lambda l, r, j: (l, base // half + 2 * j, r)),
            pl.BlockSpec((None, half, _CAST_K),
                         lambda l, r, j: (l, base // half + 1 + 2 * j, r)),
            pl.BlockSpec((None, LANES, _CAST_K),
                         lambda l, r, j: (l, (base + TN) // LANES + (TN // LANES) * j, r)),
        ],
        out_specs=pl.BlockSpec((None, _CAST_K, TN), lambda l, r, j: (l, r, j)),
        compiler_params=pltpu.CompilerParams(
            dimension_semantics=("parallel", "parallel", "parallel"),
            vmem_limit_bytes=_vmem_limit(blocks, 3 * _nbytes((TN + LANES, _CAST_K), F32))),
        name="cast_main",
    )(w_in_t, w_in_t, w_in_t)


def _rope_tables():
    pos = jnp.concatenate([
        jnp.tile(jnp.arange(SEQ, dtype=F32) + N_META, BATCH),
        jnp.arange(N_META, dtype=F32)])
    inv = 1.0 / (ROPE_THETA ** (jnp.arange(0, ROPE, 2, dtype=F32) / ROPE))
    ang = pos[:, None] * inv[None, :]
    zeros = jnp.zeros((N_ROWS, LANES - ROPE), F32)
    cos = jnp.concatenate([jnp.cos(ang), jnp.cos(ang), zeros], axis=1)
    sin = jnp.concatenate([-jnp.sin(ang), jnp.sin(ang), zeros], axis=1)
    return cos, sin


def _pad_heads(w, width):
    lead = w.shape[:-1]
    w = w.reshape(lead + (HEADS, width))
    w = jnp.pad(w, [(0, 0)] * len(lead) + [(0, 0), (0, QK_PAD - width)])
    return w.reshape(lead + (HEADS * QK_PAD,))


def kernel(x, meta_tokens, norm_g, w_in, q_lora_g, kv_lora_g, w_uq, w_ukv, q_head_g, k_head_g,
           w_pool, pool_scale, w_branch_a, w_branch_b, w_out):
    assert x.shape == (BATCH, SEQ, D_MODEL) and x.dtype == F32
    assert w_in.shape == (DEPTH, D_MODEL, LAT_SRC + MAIN_W)

    w_in_t = jnp.swapaxes(w_in, 1, 2)
    w_lat = _cast_latent_weights(w_in_t)
    w_main = _cast_main_weights(w_in_t)
    w_uq_p = _pad_heads(w_uq, QK).astype(BF16)
    w_ukv_b = w_ukv.astype(BF16)
    w_pool_b = w_pool.astype(BF16)
    w_a = w_branch_a.astype(BF16)
    w_b = w_branch_b.astype(BF16)
    w_o = w_out.astype(BF16)
    head_pad = ((0, 0), (0, QK_PAD - QK))
    gq = (jnp.pad(q_head_g, head_pad) * (QK ** -0.5 * LOG2_E)).reshape(DEPTH, 1, QK_PAD)
    gk = jnp.pad(k_head_g, head_pad).reshape(DEPTH, 1, QK_PAD)
    norm_g3 = norm_g.reshape(DEPTH, 1, D_MODEL)
    gql = q_lora_g.reshape(DEPTH, 1, Q_LORA)
    gkvl = kv_lora_g.reshape(DEPTH, 1, KV_LORA)
    pscale = pool_scale.reshape(DEPTH, 1, POOL_W)
    cos, sin = _rope_tables()

    h = (x.reshape(N_REAL, D_MODEL), meta_tokens.astype(F32))
    for layer in range(DEPTH):
        hn = _rmsnorm(h, norm_g3, layer)
        proj = _inproj(hn, w_main, layer)
        lat = _latent(hn, w_lat, layer)
        q, k, v = _mla_prep(lat, gql, gkvl, w_uq_p, w_ukv_b, gq, gk, cos, sin, layer)
        ga = _attention(q, k, v, proj, meta_queries=False)
        ga_meta = _attention(q, k, v, proj, meta_queries=True)
        gb = _pool(proj, w_pool_b, pscale, layer, meta=False)
        gb_meta = _pool(proj, w_pool_b, pscale, layer, meta=True)
        merged = _merge(ga, ga_meta, gb, gb_meta, w_a, w_b, proj, layer)
        h = (_outproj(merged, w_o, h, layer),)
    return h[0].reshape(BATCH, SEQ, D_MODEL)
```

```python
import functools

import jax
import jax.numpy as jnp
from jax import lax
from jax.experimental import pallas as pl
from jax.experimental.pallas import tpu as pltpu

F32 = jnp.float32
BF16 = jnp.bfloat16

D_MODEL = 4096
BATCH = 4
SEQ = 4096
DEPTH = 4
CHUNK = 64
N_META = 16
HEADS = 16
NOPE = 128
ROPE = 64
QK = NOPE + ROPE
V_DIM = 128
MLA_W = HEADS * V_DIM
Q_LORA = 1024
KV_LORA = 512
POOL_W = 2048
WINDOWS = (2, 4, 8, 16)
GROUPS = len(WINDOWS)
GROUP_W = POOL_W // GROUPS
NORM_EPS = 1e-6
ROPE_THETA = 10000.0
LOG2_E = 1.4426950408889634

LANES = 128
MXU_COLS = 256
VMEM_BYTES = 64 * 1024 * 1024

N_REAL = BATCH * SEQ
N_ROWS = N_REAL + N_META
META_BLK = N_REAL // N_META
QK_PAD = MXU_COLS
LAT_W = Q_LORA + KV_LORA + LANES
MAIN_W = POOL_W + MLA_W + POOL_W + 2 * D_MODEL
LAT_SRC = Q_LORA + KV_LORA + ROPE

TM_MM = 656
TM_ROW = 400
TN = 1024
TQ = 512
TP = 256


def _vmem_limit(block_bytes, temp_bytes):
    need = 2 * sum(block_bytes) + temp_bytes
    assert need <= VMEM_BYTES - (4 << 20), need
    return int(need)


def _nbytes(shape, dtype):
    n = 1
    for s in shape:
        n *= s
    return n * jnp.dtype(dtype).itemsize


def _row_tiles(arrays, emit):
    split = [refs for refs in arrays if len(refs) == 2]
    if not split:
        emit(*[refs[0][...] for refs in arrays])
        return
    tile_rows = split[0][0].shape[0]
    n_real = N_REAL % tile_rows
    assert n_real + N_META == tile_rows

    def last_tile(refs):
        if len(refs) == 1:
            return refs[0][...]
        return jnp.concatenate([refs[0][0:n_real, :], refs[1][...]], axis=0)

    last = pl.num_programs(0) - 1
    pl.when(pl.program_id(0) != last)(lambda: emit(*[refs[0][...] for refs in arrays]))
    pl.when(pl.program_id(0) == last)(lambda: emit(*[last_tile(refs) for refs in arrays]))


def _rmsnorm_kernel(*refs):
    *h_refs, g_ref, o_ref = refs

    def emit(x):
        ms = jnp.mean(x * x, axis=-1, keepdims=True)
        o_ref[...] = (x * lax.rsqrt(ms + NORM_EPS) * g_ref[...]).astype(o_ref.dtype)

    _row_tiles([tuple(h_refs)], emit)


def _rmsnorm(h_parts, norm_g, layer):
    n_tiles = N_ROWS // TM_ROW
    blocks = [_nbytes((TM_ROW, D_MODEL), F32), _nbytes((TM_ROW, D_MODEL), BF16),
              _nbytes((N_META, D_MODEL), F32)]
    h_specs = [pl.BlockSpec((TM_ROW, D_MODEL), lambda i: (i, 0))]
    if len(h_parts) == 2:
        h_specs.append(pl.BlockSpec((N_META, D_MODEL), lambda i: (0, 0)))
    return pl.pallas_call(
        _rmsnorm_kernel,
        out_shape=jax.ShapeDtypeStruct((N_ROWS, D_MODEL), BF16),
        grid=(n_tiles,),
        in_specs=h_specs + [pl.BlockSpec((None, 1, D_MODEL), lambda i: (layer, 0, 0))],
        out_specs=pl.BlockSpec((TM_ROW, D_MODEL), lambda i: (i, 0)),
        compiler_params=pltpu.CompilerParams(
            dimension_semantics=("parallel",),
            vmem_limit_bytes=_vmem_limit(blocks, 2 * _nbytes((TM_ROW, D_MODEL), F32))),
        name="rmsnorm",
    )(*h_parts, norm_g)


_RAW_TILES = POOL_W // TN
_SILU_TILES = (MLA_W + POOL_W) // TN


def _inproj_kernel(x_ref, w_ref, o_ref):
    j = pl.program_id(1)

    def sigmoid(a):
        return 0.5 * jnp.tanh(0.5 * a) + 0.5

    def tile(activation):
        acc = jnp.dot(x_ref[...], w_ref[...], preferred_element_type=F32)
        o_ref[...] = activation(acc).astype(o_ref.dtype)

    pl.when(j < _RAW_TILES)(lambda: tile(lambda a: a))
    pl.when((j >= _RAW_TILES) & (j < _RAW_TILES + _SILU_TILES))(
        lambda: tile(lambda a: a * sigmoid(a)))
    pl.when(j >= _RAW_TILES + _SILU_TILES)(lambda: tile(sigmoid))


def _inproj(hn, w_main, layer):
    blocks = [_nbytes((TM_MM, D_MODEL), BF16), _nbytes((D_MODEL, TN), BF16),
              _nbytes((TM_MM, TN), BF16)]
    return pl.pallas_call(
        _inproj_kernel,
        out_shape=jax.ShapeDtypeStruct((N_ROWS, MAIN_W), BF16),
        grid=(N_ROWS // TM_MM, MAIN_W // TN),
        in_specs=[
            pl.BlockSpec((TM_MM, D_MODEL), lambda i, j: (i, 0)),
            pl.BlockSpec((None, D_MODEL, TN), lambda i, j: (layer, 0, j)),
        ],
        out_specs=pl.BlockSpec((TM_MM, TN), lambda i, j: (i, j)),
        compiler_params=pltpu.CompilerParams(
            dimension_semantics=("parallel", "parallel"),
            vmem_limit_bytes=_vmem_limit(blocks, 3 * _nbytes((TM_MM, TN), F32))),
        name="inproj",
    )(hn, w_main)


def _latent_kernel(x_ref, w_ref, o_ref):
    o_ref[...] = jnp.dot(x_ref[...], w_ref[...], preferred_element_type=F32)


def _latent(hn, w_lat, layer):
    blocks = [_nbytes((TM_ROW, D_MODEL), BF16), _nbytes((D_MODEL, LAT_W), BF16),
              _nbytes((TM_ROW, LAT_W), F32)]
    return pl.pallas_call(
        _latent_kernel,
        out_shape=jax.ShapeDtypeStruct((N_ROWS, LAT_W), F32),
        grid=(N_ROWS // TM_ROW,),
        in_specs=[
            pl.BlockSpec((TM_ROW, D_MODEL), lambda i: (i, 0)),
            pl.BlockSpec((None, D_MODEL, LAT_W), lambda i: (layer, 0, 0)),
        ],
        out_specs=pl.BlockSpec((TM_ROW, LAT_W), lambda i: (i, 0)),
        compiler_params=pltpu.CompilerParams(
            dimension_semantics=("parallel",),
            vmem_limit_bytes=_vmem_limit(blocks, _nbytes((TM_ROW, LAT_W), F32))),
        name="latent",
    )(hn, w_lat)


def _rope(x, cos, sin_signed):
    partner = pltpu.roll(x, ROPE // 2, axis=1) + pltpu.roll(x, LANES - ROPE // 2, axis=1)
    return x * cos + partner * sin_signed


def _mla_prep_kernel(lat_ref, gql_ref, gkvl_ref, wuq_ref, wukv_ref, gq_ref, gk_ref,
                     cos_ref, sin_ref, q_ref, k_ref, v_ref):
    cos = cos_ref[...]
    sin = sin_ref[...]
    cq = lat_ref[:, 0:Q_LORA]
    ckv = lat_ref[:, Q_LORA:Q_LORA + KV_LORA]
    kr = lat_ref[:, Q_LORA + KV_LORA:LAT_W]
    cqn = (cq * lax.rsqrt(jnp.mean(cq * cq, axis=-1, keepdims=True) + NORM_EPS)
           * gql_ref[...]).astype(BF16)
    ckvn = (ckv * lax.rsqrt(jnp.mean(ckv * ckv, axis=-1, keepdims=True) + NORM_EPS)
            * gkvl_ref[...]).astype(BF16)
    gq = gq_ref[...]
    gk = gk_ref[...]
    kr_ssq = jnp.sum(kr * kr, axis=-1, keepdims=True)
    kr_roped = _rope(kr * gk[:, NOPE:QK_PAD], cos, sin)
    for h in range(HEADS):
        c0 = h * QK_PAD
        qh = jnp.dot(cqn, wuq_ref[:, c0:c0 + QK_PAD], preferred_element_type=F32)
        rq = lax.rsqrt(jnp.sum(qh * qh, axis=-1, keepdims=True) * (1.0 / QK) + NORM_EPS)
        qn = qh * rq * gq
        q_ref[:, c0:c0 + NOPE] = qn[:, 0:NOPE].astype(BF16)
        q_ref[:, c0 + NOPE:c0 + QK_PAD] = _rope(qn[:, NOPE:QK_PAD], cos, sin).astype(BF16)
        kvh = jnp.dot(ckvn, wukv_ref[:, c0:c0 + QK_PAD], preferred_element_type=F32)
        kn = kvh[:, 0:NOPE]
        rk = lax.rsqrt((jnp.sum(kn * kn, axis=-1, keepdims=True) + kr_ssq) * (1.0 / QK)
                       + NORM_EPS)
        k_ref[:, c0:c0 + NOPE] = (kn * rk * gk[:, 0:NOPE]).astype(BF16)
        k_ref[:, c0 + NOPE:c0 + QK_PAD] = (kr_roped * rk).astype(BF16)
        v_ref[:, h * V_DIM:(h + 1) * V_DIM] = kvh[:, NOPE:NOPE + V_DIM].astype(BF16)


def _mla_prep(lat, gql, gkvl, wuq, wukv, gq, gk, cos, sin, layer):
    blocks = [_nbytes((TM_ROW, LAT_W), F32), _nbytes((Q_LORA, HEADS * QK_PAD), BF16),
              _nbytes((KV_LORA, HEADS * QK_PAD), BF16),
              2 * _nbytes((TM_ROW, HEADS * QK_PAD), BF16), _nbytes((TM_ROW, MLA_W), BF16),
              2 * _nbytes((TM_ROW, LANES), F32)]
    row = lambda i: (i, 0)
    par = lambda i: (layer, 0, 0)
    return pl.pallas_call(
        _mla_prep_kernel,
        out_shape=(jax.ShapeDtypeStruct((N_ROWS, HEADS * QK_PAD), BF16),
                   jax.ShapeDtypeStruct((N_ROWS, HEADS * QK_PAD), BF16),
                   jax.ShapeDtypeStruct((N_ROWS, MLA_W), BF16)),
        grid=(N_ROWS // TM_ROW,),
        in_specs=[
            pl.BlockSpec((TM_ROW, LAT_W), row),
            pl.BlockSpec((None, 1, Q_LORA), par),
            pl.BlockSpec((None, 1, KV_LORA), par),
            pl.BlockSpec((None, Q_LORA, HEADS * QK_PAD), par),
            pl.BlockSpec((None, KV_LORA, HEADS * QK_PAD), par),
            pl.BlockSpec((None, 1, QK_PAD), par),
            pl.BlockSpec((None, 1, QK_PAD), par),
            pl.BlockSpec((TM_ROW, LANES), row),
            pl.BlockSpec((TM_ROW, LANES), row),
        ],
        out_specs=(pl.BlockSpec((TM_ROW, HEADS * QK_PAD), row),
                   pl.BlockSpec((TM_ROW, HEADS * QK_PAD), row),
                   pl.BlockSpec((TM_ROW, MLA_W), row)),
        compiler_params=pltpu.CompilerParams(
            dimension_semantics=("parallel",),
            vmem_limit_bytes=_vmem_limit(
                blocks, _nbytes((TM_ROW, Q_LORA + KV_LORA), BF16)
                + 6 * _nbytes((TM_ROW, QK_PAD), F32))),
        name="mla_prep",
    )(lat, gql, gkvl, wuq, wukv, gq, gk, cos, sin)


_NT = (((1,), (1,)), ((), ()))
_MASKED = -1e30


def _row_max(x):
    return jnp.max(x, axis=-1, keepdims=True)


def _with_ones(v):
    lane = lax.broadcasted_iota(jnp.int32, (v.shape[0], LANES), 1)
    return jnp.concatenate([v, (lane == 0).astype(v.dtype)], axis=1)


def _tile_update(s, v, state, meta=None):
    row_max = _row_max(s)
    if meta is not None:
        row_max = jnp.maximum(row_max, _row_max(meta[0]))
    m_new = row_max if state is None else jnp.maximum(state[0], row_max)
    pv = jnp.dot(jnp.exp2(s - m_new).astype(BF16), v, preferred_element_type=F32)
    if meta is not None:
        p_meta = jnp.exp2(meta[0] - m_new).astype(BF16)
        pv = pv + jnp.dot(p_meta, meta[1], preferred_element_type=F32)
    if state is None:
        return m_new, pv
    return m_new, jnp.exp2(state[0] - m_new) * state[1] + pv


def _causal_bias():
    q_chunk = jnp.arange(TQ)[:, None] // CHUNK
    k_chunk = jnp.arange(TQ)[None, :] // CHUNK
    return jnp.where(k_chunk <= q_chunk, 0.0, _MASKED).astype(F32)


def _attn_kernel(q_ref, k_ref, v_ref, km_ref, vm_ref, g_ref, bias_ref, o_ref, *, real_keys):
    pad = LANES - N_META
    km = jnp.concatenate([km_ref[...], jnp.zeros((pad, QK_PAD), BF16)], axis=0)
    vm = _with_ones(jnp.concatenate([vm_ref[...], jnp.zeros((pad, V_DIM), BF16)], axis=0))

    def meta_scores(q):
        s = lax.dot_general(q, km, _NT, preferred_element_type=F32)
        return jnp.where(lax.broadcasted_iota(jnp.int32, s.shape, 1) < N_META, s, _MASKED)

    def finish(rows, acc):
        out = acc[:, 0:V_DIM] / acc[:, V_DIM:V_DIM + 1]
        o_ref[rows, :] = (out * g_ref[rows, :].astype(F32)).astype(o_ref.dtype)

    def chain(tile):
        rows = slice(tile * TQ, (tile + 1) * TQ)
        q = q_ref[rows, :]
        order = list(range(tile))
        order.insert(tile // 2, tile)

        def scores(j):
            s = lax.dot_general(q, k_ref[j * TQ:(j + 1) * TQ, :], _NT,
                                preferred_element_type=F32)
            return s + bias_ref[...] if j == tile else s

        meta = (meta_scores(q), vm)
        state = None
        s_next = scores(order[0])
        for n, j in enumerate(order):
            if n > 0:
                yield
            s = s_next
            if n + 1 < len(order):
                s_next = scores(order[n + 1])
            state = _tile_update(s, _with_ones(v_ref[j * TQ:(j + 1) * TQ, :]), state,
                                 meta=meta if n == 0 else None)
        finish(rows, state[1])

    def pair(short):
        lengths = [SEQ // TQ - short, short + 1]
        chains = [chain(SEQ // TQ - 1 - short), chain(short)]
        done = [0, 0]
        while chains[0] is not None or chains[1] is not None:
            live = [c for c in (0, 1) if chains[c] is not None]
            c = min(live, key=lambda c: (done[c] + 0.5) / lengths[c])
            try:
                next(chains[c])
                done[c] += 1
            except StopIteration:
                chains[c] = None

    if real_keys:
        step = pl.program_id(2)
        for short in range(SEQ // TQ // 2):
            pl.when(step == short)(functools.partial(pair, short))
    else:
        s = meta_scores(q_ref[...])
        p = jnp.exp2(s - _row_max(s)).astype(BF16)
        finish(slice(None), jnp.dot(p, vm, preferred_element_type=F32))


_GATE_A_BLK = POOL_W // V_DIM


def _attention(q, k, v, proj, meta_queries):
    if not meta_queries:
        blocks = [2 * _nbytes((SEQ, QK_PAD), BF16), 3 * _nbytes((SEQ, V_DIM), BF16),
                  _nbytes((N_META, QK_PAD + V_DIM), BF16), _nbytes((TQ, TQ), F32)]
        head = lambda b, h, step: (b, h)
        return pl.pallas_call(
            functools.partial(_attn_kernel, real_keys=True),
            out_shape=jax.ShapeDtypeStruct((N_REAL, MLA_W), BF16),
            grid=(BATCH, HEADS, SEQ // TQ // 2),
            in_specs=[
                pl.BlockSpec((SEQ, QK_PAD), head),
                pl.BlockSpec((SEQ, QK_PAD), head),
                pl.BlockSpec((SEQ, V_DIM), head),
                pl.BlockSpec((N_META, QK_PAD), lambda b, h, step: (META_BLK, h)),
                pl.BlockSpec((N_META, V_DIM), lambda b, h, step: (META_BLK, h)),
                pl.BlockSpec((SEQ, V_DIM), lambda b, h, step: (b, _GATE_A_BLK + h)),
                pl.BlockSpec((TQ, TQ), lambda b, h, step: (0, 0)),
            ],
            out_specs=pl.BlockSpec((SEQ, V_DIM), head),
            compiler_params=pltpu.CompilerParams(
                dimension_semantics=("parallel", "parallel", "arbitrary"),
                vmem_limit_bytes=_vmem_limit(blocks, 16 * _nbytes((TQ, TQ), F32))),
            name="attention",
        )(q, k, v, k, v, proj, _causal_bias())
    blocks = [_nbytes((N_META, QK_PAD), BF16), _nbytes((N_META, QK_PAD + V_DIM), BF16),
              2 * _nbytes((N_META, V_DIM), BF16)]
    meta = lambda h: (META_BLK, h)
    return pl.pallas_call(
        lambda q_ref, km_ref, vm_ref, g_ref, o_ref: _attn_kernel(
            q_ref, None, None, km_ref, vm_ref, g_ref, None, o_ref, real_keys=False),
        out_shape=jax.ShapeDtypeStruct((N_META, MLA_W), BF16),
        grid=(HEADS,),
        in_specs=[
            pl.BlockSpec((N_META, QK_PAD), meta),
            pl.BlockSpec((N_META, QK_PAD), meta),
            pl.BlockSpec((N_META, V_DIM), meta),
            pl.BlockSpec((N_META, V_DIM), lambda h: (META_BLK, _GATE_A_BLK + h)),
        ],
        out_specs=pl.BlockSpec((N_META, V_DIM), lambda h: (0, h)),
        compiler_params=pltpu.CompilerParams(
            dimension_semantics=("parallel",),
            vmem_limit_bytes=_vmem_limit(blocks, 8 * _nbytes((LANES, QK_PAD), F32))),
        name="attention_meta",
    )(q, k, v, proj)


def _pool_kernel(u_ref, halo_ref, bm_ref, bh_ref, invc_ref, wp_ref, sc_ref, g_ref, o_ref,
                 *, has_main):
    pad = LANES - N_META
    for g in range(GROUPS):
        cs = slice(g * GROUP_W, (g + 1) * GROUP_W)
        halo = jnp.concatenate([halo_ref[:, cs], jnp.zeros((pad, GROUP_W), BF16)], axis=0)
        win = jnp.dot(bh_ref[g], halo, preferred_element_type=F32)
        u = u_ref[:, cs]
        if has_main:
            main = jnp.dot(bm_ref[g], u, preferred_element_type=F32)
            win = jnp.concatenate([main[0:N_META] + win, main[N_META:]], axis=0)
        diff = win * invc_ref[g][:, 0:1] - u.astype(F32)
        mixed = jnp.dot(diff.astype(BF16), wp_ref[g], preferred_element_type=F32)
        out = mixed * sc_ref[:, cs] * g_ref[:, cs].astype(F32)
        o_ref[:, cs] = out.astype(o_ref.dtype)


_GATE_B_BLK = (POOL_W + MLA_W) // POOL_W


def _pool_bands(rows, meta):
    t = jnp.arange(rows)[:, None]
    s = jnp.arange(rows)[None, :]
    t_h = jnp.arange(N_META)[:, None]
    s_h = jnp.arange(LANES)[None, :]
    main, halo, invc = [], [], []
    for w in WINDOWS:
        main.append(((t - s >= 0) & (t - s < w)).astype(BF16))
        if meta:
            halo.append(((t_h - s_h >= 0) & (t_h - s_h < w) & (s_h < N_META)).astype(BF16))
            cnt = jnp.minimum(t + 1, w).astype(F32)
        else:
            dist = t_h + N_META - s_h
            halo.append(((dist < w) & (s_h < N_META)).astype(BF16))
            cnt = jnp.full((rows, 1), w, F32)
        invc.append(jnp.broadcast_to(1.0 / cnt, (rows, LANES)))
    return jnp.stack(main), jnp.stack(halo), jnp.stack(invc)


def _pool(proj, w_pool, pool_scale, layer, meta):
    rows = N_META if meta else TP
    bm, bh, invc = _pool_bands(rows, meta)
    nt = SEQ // TP
    if meta:
        grid = (1,)
        tile = lambda t: (META_BLK, 0)
        halo = tile
        gate = lambda t: (META_BLK, _GATE_B_BLK)
        out = lambda t: (0, 0)
        const3 = lambda t: (0, 0, 0)
        wmap = lambda t: (layer, 0, 0, 0)
        smap = lambda t: (layer, 0, 0)
    else:
        grid = (BATCH, nt)
        tile = lambda b, t: (b * nt + t, 0)
        halo = lambda b, t: (jnp.where(t == 0, META_BLK,
                                       (b * SEQ + t * TP) // N_META - 1), 0)
        gate = lambda b, t: (b * nt + t, _GATE_B_BLK)
        out = tile
        const3 = lambda b, t: (0, 0, 0)
        wmap = lambda b, t: (layer, 0, 0, 0)
        smap = lambda b, t: (layer, 0, 0)
    blocks = [3 * _nbytes((rows, POOL_W), BF16), _nbytes((N_META, POOL_W), BF16),
              _nbytes((GROUPS, rows, rows), BF16), _nbytes((GROUPS, N_META, LANES), BF16),
              _nbytes((GROUPS, rows, LANES), F32),
              _nbytes((GROUPS, GROUP_W, GROUP_W), BF16), _nbytes((1, POOL_W), F32)]
    return pl.pallas_call(
        functools.partial(_pool_kernel, has_main=not meta),
        out_shape=jax.ShapeDtypeStruct((N_META if meta else N_REAL, POOL_W), BF16),
        grid=grid,
        in_specs=[
            pl.BlockSpec((rows, POOL_W), tile),
            pl.BlockSpec((N_META, POOL_W), halo),
            pl.BlockSpec((GROUPS, rows, rows), const3),
            pl.BlockSpec((GROUPS, N_META, LANES), const3),
            pl.BlockSpec((GROUPS, rows, LANES), const3),
            pl.BlockSpec((None, GROUPS, GROUP_W, GROUP_W), wmap),
            pl.BlockSpec((None, 1, POOL_W), smap),
            pl.BlockSpec((rows, POOL_W), gate),
        ],
        out_specs=pl.BlockSpec((rows, POOL_W), out),
        compiler_params=pltpu.CompilerParams(
            dimension_semantics=("parallel",) * len(grid),
            vmem_limit_bytes=_vmem_limit(blocks, 6 * _nbytes((max(rows, LANES), GROUP_W), F32))),
        name="pool_meta" if meta else "pool",
    )(proj, proj, bm, bh, invc, w_pool, pool_scale, proj)


def _merge_kernel(ga_ref, ga_meta_ref, gb_ref, gb_meta_ref, wa_ref, wb_ref, sa_ref, sb_ref,
                  o_ref):
    def emit(ga, gb):
        a = jnp.dot(ga, wa_ref[...], preferred_element_type=F32)
        b = jnp.dot(gb, wb_ref[...], preferred_element_type=F32)
        o_ref[...] = (sa_ref[...].astype(F32) * a
                      + sb_ref[...].astype(F32) * b).astype(o_ref.dtype)

    _row_tiles([(ga_ref, ga_meta_ref), (gb_ref, gb_meta_ref)], emit)


_MERGE_A_BLK = (POOL_W + MLA_W + POOL_W) // TN
_MERGE_B_BLK = _MERGE_A_BLK + D_MODEL // TN


def _merge(ga, ga_meta, gb, gb_meta, w_a, w_b, proj, layer):
    blocks = [2 * _nbytes((TM_MM, MLA_W), BF16), 2 * _nbytes((MLA_W, TN), BF16),
              3 * _nbytes((TM_MM, TN), BF16), 2 * _nbytes((N_META, MLA_W), BF16)]
    return pl.pallas_call(
        _merge_kernel,
        out_shape=jax.ShapeDtypeStruct((N_ROWS, D_MODEL), BF16),
        grid=(N_ROWS // TM_MM, D_MODEL // TN),
        in_specs=[
            pl.BlockSpec((TM_MM, MLA_W), lambda i, j: (i, 0)),
            pl.BlockSpec((N_META, MLA_W), lambda i, j: (0, 0)),
            pl.BlockSpec((TM_MM, POOL_W), lambda i, j: (i, 0)),
            pl.BlockSpec((N_META, POOL_W), lambda i, j: (0, 0)),
            pl.BlockSpec((None, MLA_W, TN), lambda i, j: (layer, 0, j)),
            pl.BlockSpec((None, POOL_W, TN), lambda i, j: (layer, 0, j)),
            pl.BlockSpec((TM_MM, TN), lambda i, j: (i, _MERGE_A_BLK + j)),
            pl.BlockSpec((TM_MM, TN), lambda i, j: (i, _MERGE_B_BLK + j)),
        ],
        out_specs=pl.BlockSpec((TM_MM, TN), lambda i, j: (i, j)),
        compiler_params=pltpu.CompilerParams(
            dimension_semantics=("parallel", "parallel"),
            vmem_limit_bytes=_vmem_limit(blocks, 3 * _nbytes((TM_MM, TN), F32)
                                         + 2 * _nbytes((TM_MM, MLA_W), BF16))),
        name="merge",
    )(ga, ga_meta, gb, gb_meta, w_a, w_b, proj, proj)


def _outproj_kernel(m_ref, w_ref, *refs):
    *h_refs, o_ref = refs
    delta = jnp.dot(m_ref[...], w_ref[...], preferred_element_type=F32)

    def emit(h):
        o_ref[...] = h + delta

    _row_tiles([tuple(h_refs)], emit)


def _outproj(merged, w_out, h_parts, layer):
    blocks = [_nbytes((TM_MM, D_MODEL), BF16), _nbytes((D_MODEL, TN), BF16),
              2 * _nbytes((TM_MM, TN), F32), _nbytes((N_META, TN), F32)]
    out_rows = N_REAL if layer == DEPTH - 1 else N_ROWS
    h_specs = [pl.BlockSpec((TM_MM, TN), lambda i, j: (i, j))]
    if len(h_parts) == 2:
        h_specs.append(pl.BlockSpec((N_META, TN), lambda i, j: (0, j)))
    in_place = len(h_parts) == 1 and out_rows == N_ROWS
    return pl.pallas_call(
        _outproj_kernel,
        out_shape=jax.ShapeDtypeStruct((out_rows, D_MODEL), F32),
        grid=(N_ROWS // TM_MM, D_MODEL // TN),
        in_specs=[
            pl.BlockSpec((TM_MM, D_MODEL), lambda i, j: (i, 0)),
            pl.BlockSpec((None, D_MODEL, TN), lambda i, j: (layer, 0, j)),
        ] + h_specs,
        out_specs=pl.BlockSpec((TM_MM, TN), lambda i, j: (i, j)),
        input_output_aliases={2: 0} if in_place else {},
        compiler_params=pltpu.CompilerParams(
            dimension_semantics=("parallel", "parallel"),
            vmem_limit_bytes=_vmem_limit(blocks, 4 * _nbytes((TM_MM, TN), F32))),
        name="outproj",
    )(merged, w_out, *h_parts)


_CAST_K = 1024
_SHIFT = LAT_SRC % LANES


def _cast_main_kernel(a_ref, b_ref, c_ref, o_ref):
    window = jnp.concatenate([a_ref[...], b_ref[...], c_ref[...]], axis=0)
    o_ref[...] = window[_SHIFT:_SHIFT + TN, :].T.astype(o_ref.dtype)


def _cast_latent_kernel(w_ref, o_ref):
    col = lax.broadcasted_iota(jnp.int32, w_ref.shape, 0)
    o_ref[...] = jnp.where(col < LAT_SRC, w_ref[...], 0.0).T.astype(o_ref.dtype)


def _cast_latent_weights(w_in_t):
    blocks = [_nbytes((LAT_W, _CAST_K), F32), _nbytes((_CAST_K, LAT_W), BF16)]
    return pl.pallas_call(
        _cast_latent_kernel,
        out_shape=jax.ShapeDtypeStruct((DEPTH, D_MODEL, LAT_W), BF16),
        grid=(DEPTH, D_MODEL // _CAST_K),
        in_specs=[pl.BlockSpec((None, LAT_W, _CAST_K), lambda l, r: (l, 0, r))],
        out_specs=pl.BlockSpec((None, _CAST_K, LAT_W), lambda l, r: (l, r, 0)),
        compiler_params=pltpu.CompilerParams(
            dimension_semantics=("parallel", "parallel"),
            vmem_limit_bytes=_vmem_limit(blocks, 3 * _nbytes((_CAST_K, LAT_W), F32))),
        name="cast_latent",
    )(w_in_t)


def _cast_main_weights(w_in_t):
    base = LAT_SRC - _SHIFT
    half = TN // 2
    assert base % half == 0 and (base + TN) % LANES == 0
    blocks = [2 * _nbytes((half, _CAST_K), F32), _nbytes((LANES, _CAST_K), F32),
              _nbytes((_CAST_K, TN), BF16)]
    return pl.pallas_call(
        _cast_main_kernel,
        out_shape=jax.ShapeDtypeStruct((DEPTH, D_MODEL, MAIN_W), BF16),
        grid=(DEPTH, D_MODEL // _CAST_K, MAIN_W // TN),
        in_specs=[
            pl.BlockSpec((None, half, _CAST_K), lambda l, r, j: (l, base // half + 2 * j, r)),
            pl.BlockSpec((None, half, _CAST_K),
                         lambda l, r, j: (l, base // half + 1 + 2 * j, r)),
            pl.BlockSpec((None, LANES, _CAST_K),
                         lambda l, r, j: (l, (base + TN) // LANES + (TN // LANES) * j, r)),
        ],
        out_specs=pl.BlockSpec((None, _CAST_K, TN), lambda l, r, j: (l, r, j)),
        compiler_params=pltpu.CompilerParams(
            dimension_semantics=("parallel", "parallel", "parallel"),
            vmem_limit_bytes=_vmem_limit(blocks, 3 * _nbytes((TN + LANES, _CAST_K), F32))),
        name="cast_main",
    )(w_in_t, w_in_t, w_in_t)


def _rope_tables():
    pos = jnp.concatenate([
        jnp.tile(jnp.arange(SEQ, dtype=F32) + N_META, BATCH),
        jnp.arange(N_META, dtype=F32)])
    inv = 1.0 / (ROPE_THETA ** (jnp.arange(0, ROPE, 2, dtype=F32) / ROPE))
    ang = pos[:, None] * inv[None, :]
    zeros = jnp.zeros((N_ROWS, LANES - ROPE), F32)
    cos = jnp.concatenate([jnp.cos(ang), jnp.cos(ang), zeros], axis=1)
    sin = jnp.concatenate([-jnp.sin(ang), jnp.sin(ang), zeros], axis=1)
    return cos, sin


def _pad_heads(w, width):
    lead = w.shape[:-1]
    w = w.reshape(lead + (HEADS, width))
    w = jnp.pad(w, [(0, 0)] * len(lead) + [(0, 0), (0, QK_PAD - width)])
    return w.reshape(lead + (HEADS * QK_PAD,))


def kernel(x, meta_tokens, norm_g, w_in, q_lora_g, kv_lora_g, w_uq, w_ukv, q_head_g, k_head_g,
           w_pool, pool_scale, w_branch_a, w_branch_b, w_out):
    assert x.shape == (BATCH, SEQ, D_MODEL) and x.dtype == F32
    assert w_in.shape == (DEPTH, D_MODEL, LAT_SRC + MAIN_W)

    w_in_t = jnp.swapaxes(w_in, 1, 2)
    w_lat = _cast_latent_weights(w_in_t)
    w_main = _cast_main_weights(w_in_t)
    w_uq_p = _pad_heads(w_uq, QK).astype(BF16)
    w_ukv_b = w_ukv.astype(BF16)
    w_pool_b = w_pool.astype(BF16)
    w_a = w_branch_a.astype(BF16)
    w_b = w_branch_b.astype(BF16)
    w_o = w_out.astype(BF16)
    head_pad = ((0, 0), (0, QK_PAD - QK))
    gq = (jnp.pad(q_head_g, head_pad) * (QK ** -0.5 * LOG2_E)).reshape(DEPTH, 1, QK_PAD)
    gk = jnp.pad(k_head_g, head_pad).reshape(DEPTH, 1, QK_PAD)
    norm_g3 = norm_g.reshape(DEPTH, 1, D_MODEL)
    gql = q_lora_g.reshape(DEPTH, 1, Q_LORA)
    gkvl = kv_lora_g.reshape(DEPTH, 1, KV_LORA)
    pscale = pool_scale.reshape(DEPTH, 1, POOL_W)
    cos, sin = _rope_tables()

    h = (x.reshape(N_REAL, D_MODEL), meta_tokens.astype(F32))
    for layer in range(DEPTH):
        hn = _rmsnorm(h, norm_g3, layer)
        proj = _inproj(hn, w_main, layer)
        lat = _latent(hn, w_lat, layer)
        q, k, v = _mla_prep(lat, gql, gkvl, w_uq_p, w_ukv_b, gq, gk, cos, sin, layer)
        ga = _attention(q, k, v, proj, meta_queries=False)
        ga_meta = _attention(q, k, v, proj, meta_queries=True)
        gb = _pool(proj, w_pool_b, pscale, layer, meta=False)
        gb_meta = _pool(proj, w_pool_b, pscale, layer, meta=True)
        merged = _merge(ga, ga_meta, gb, gb_meta, w_a, w_b, proj, layer)
        h = (_outproj(merged, w_o, h, layer),)
    return h[0].reshape(BATCH, SEQ, D_MODEL)
```
